```python
import math
import jax, jax.numpy as jnp
from jax import lax
import numpy as np

D_MODEL = 1024
BATCH = 4
SEQ = 8192
DEPTH = 1

NSA_HEADS = 8
NSA_KV_GROUPS = 2
NSA_HEAD_DIM = 64
NSA_REP = NSA_HEADS // NSA_KV_GROUPS
N_BRANCH = 3
CMP_BLOCK = 32
CMP_STRIDE = 16
CMP_HIDDEN = 256
SLC_BLOCK = 64
SLC_TOPK = 16
WINDOW = 512
Q_BLOCK = 128
RET_HEADS = 4
RET_HEAD_DIM = 128
RET_CHUNK = 128
ROPE_BASE = 10000.0
NSA_WIDTH = NSA_HEADS * NSA_HEAD_DIM
NSA_KV_WIDTH = NSA_KV_GROUPS * NSA_HEAD_DIM
RET_WIDTH = RET_HEADS * RET_HEAD_DIM
MIX_WIDTH = NSA_WIDTH + RET_WIDTH
SPLIT_SIZES = (NSA_WIDTH, 2 * N_BRANCH * NSA_KV_WIDTH, NSA_HEADS * N_BRANCH, 3 * RET_WIDTH, RET_WIDTH)
IN_WIDTH = NSA_WIDTH + 2 * N_BRANCH * NSA_KV_WIDTH + NSA_HEADS * N_BRANCH + 3 * RET_WIDTH + RET_WIDTH
D_FF = 2816
PLE_DIM = 256
N_LN = 4
ALPHA = (2.0 * DEPTH) ** 0.25
BETA = (8.0 * DEPTH) ** -0.25
LN_EPS = 1e-5
NEG = -1e30

kernel_name = "hybrid_nsa_retention_macaron_deepnorm"


def _layer_norm(x, g, b):
    xf = x.astype(jnp.float32)
    mu = jnp.mean(xf, axis=-1, keepdims=True)
    var = jnp.mean(jnp.square(xf - mu), axis=-1, keepdims=True)
    y = (xf - mu) * lax.rsqrt(var + LN_EPS)
    return (y * g.astype(jnp.float32) + b.astype(jnp.float32)).astype(x.dtype)


def _swiglu(x, w13, w2):
    a, u = jnp.split(x @ w13, 2, axis=-1)
    return (jax.nn.silu(a) * u) @ w2


def _rotary(x, pos):
    d = x.shape[-1]
    freqs = ROPE_BASE ** (-jnp.arange(0, d, 2, dtype=jnp.float32) / d)
    ang = pos[:, None] * freqs[None, :]
    cos = jnp.cos(ang)[None, :, None, :]
    sin = jnp.sin(ang)[None, :, None, :]
    x1, x2 = jnp.split(x, 2, axis=-1)
    out = jnp.concatenate([x1 * cos - x2 * sin, x1 * sin + x2 * cos], axis=-1)
    return out.astype(x.dtype)


def _compress(kraw, pos_emb, w1, w2):
    B, G, T, dh = kraw.shape
    n_cmp = (T - CMP_BLOCK) // CMP_STRIDE + 1
    idx = jnp.arange(n_cmp)[:, None] * CMP_STRIDE + jnp.arange(CMP_BLOCK)[None, :]
    blocks = kraw[:, :, idx, :] + pos_emb
    flat = blocks.reshape(B, G, n_cmp, CMP_BLOCK * dh)
    return jax.nn.gelu(flat @ w1) @ w2


def _nsa(q, gates, k_cmp, v_cmp, k_slc, v_slc, k_win, v_win):
    B, G, R, T, dh = q.shape
    n_cmp = k_cmp.shape[2]
    n_slc = T // SLC_BLOCK
    topk = min(SLC_TOPK, n_slc)
    scale = dh ** -0.5
    cmp_start = jnp.arange(n_cmp) * CMP_STRIDE
    cmp_last = cmp_start + CMP_BLOCK - 1
    slc_start = jnp.arange(n_slc) * SLC_BLOCK
    overlap = jnp.clip(jnp.minimum(cmp_start[:, None] + CMP_BLOCK, slc_start[None, :] + SLC_BLOCK)
                       - jnp.maximum(cmp_start[:, None], slc_start[None, :]), 0).astype(jnp.float32) / CMP_BLOCK
    k_blk = k_slc.reshape(B, G, n_slc, SLC_BLOCK, dh)
    v_blk = v_slc.reshape(B, G, n_slc, SLC_BLOCK, dh)
    pad = ((0, 0), (0, 0), (WINDOW, 0), (0, 0))
    k_pad = jnp.pad(k_win, pad)
    v_pad = jnp.pad(v_win, pad)
    bi = jnp.arange(B)[:, None, None, None]
    gi = jnp.arange(G)[None, :, None, None]
    blk_ids = jnp.arange(n_slc)

    def block_fn(qb):
        q0 = qb * Q_BLOCK
        qq = lax.dynamic_slice_in_dim(q, q0, Q_BLOCK, axis=3)
        gg = lax.dynamic_slice_in_dim(gates, q0, Q_BLOCK, axis=3)
        t = q0 + jnp.arange(Q_BLOCK)
        s_c = jnp.einsum('bgrqd,bgcd->bgrqc', qq, k_cmp).astype(jnp.float32) * scale
        mask_c = cmp_last[None, :] <= t[:, None]
        p_c = jax.nn.softmax(jnp.where(mask_c, s_c, NEG), axis=-1) * mask_c
        o_c = jnp.einsum('bgrqc,bgcd->bgrqd', p_c.astype(v_cmp.dtype), v_cmp)
        imp = jnp.einsum('bgrqc,cs->bgqs', p_c, overlap)
        cur = t // SLC_BLOCK
        forced = (blk_ids[None, :] == 0) | (blk_ids[None, :] == cur[:, None]) | (blk_ids[None, :] == cur[:, None] - 1)
        valid_b = slc_start[None, :] <= t[:, None]
        imp = jnp.where(valid_b, jnp.where(forced, jnp.inf, imp), -jnp.inf)
        _, sel = lax.top_k(imp, topk)
        key_pos = sel[..., None] * SLC_BLOCK + jnp.arange(SLC_BLOCK)
        mask_s = (key_pos <= t[None, None, :, None, None]).reshape(B, G, Q_BLOCK, topk * SLC_BLOCK)
        ks = k_blk[bi, gi, sel].reshape(B, G, Q_BLOCK, topk * SLC_BLOCK, dh)
        vs = v_blk[bi, gi, sel].reshape(B, G, Q_BLOCK, topk * SLC_BLOCK, dh)
        s_s = jnp.einsum('bgrqd,bgqkd->bgrqk', qq, ks).astype(jnp.float32) * scale
        p_s = jax.nn.softmax(jnp.where(mask_s[:, :, None], s_s, NEG), axis=-1)
        o_s = jnp.einsum('bgrqk,bgqkd->bgrqd', p_s.astype(vs.dtype), vs)
        kw = lax.dynamic_slice_in_dim(k_pad, q0, WINDOW + Q_BLOCK, axis=2)
        vw = lax.dynamic_slice_in_dim(v_pad, q0, WINDOW + Q_BLOCK, axis=2)
        kpos = q0 - WINDOW + jnp.arange(WINDOW + Q_BLOCK)
        dist = t[:, None] - kpos[None, :]
        mask_w = (kpos[None, :] >= 0) & (dist >= 0) & (dist < WINDOW)
        s_w = jnp.einsum('bgrqd,bgkd->bgrqk', qq, kw).astype(jnp.float32) * scale
        p_w = jax.nn.softmax(jnp.where(mask_w, s_w, NEG), axis=-1)
        o_w = jnp.einsum('bgrqk,bgkd->bgrqd', p_w.astype(vw.dtype), vw)
        out = gg[..., 0:1] * o_c + gg[..., 1:2] * o_s + gg[..., 2:3] * o_w
        return out.astype(q.dtype)

    outs = lax.map(block_fn, jnp.arange(T // Q_BLOCK))
    return outs.transpose(1, 0, 4, 2, 3, 5).reshape(B, T, G * R * dh)


def _retention(q, k, v, gn_g, gn_b):
    B, T, H, d = q.shape
    C = RET_CHUNK
    N = T // C
    gamma = 1.0 - jnp.exp2(-5.0 - jnp.arange(H, dtype=jnp.float32))
    log_g = jnp.log(gamma)
    pos = jnp.arange(T, dtype=jnp.float32)
    q = _rotary(q, pos)
    k = _rotary(k, pos) * (d ** -0.5)
    to_chunks = lambda a: a.reshape(B, N, C, H, d).transpose(0, 3, 1, 2, 4)
    qc, kc, vc = to_chunks(q), to_chunks(k), to_chunks(v)
    i = jnp.arange(C, dtype=jnp.float32)
    rel = i[:, None] - i[None, :]
    decay = jnp.where(rel[None] >= 0, jnp.exp(jnp.maximum(rel, 0.0)[None] * log_g[:, None, None]), 0.0)
    inner = jnp.einsum('bhnid,bhnjd->bhnij', qc, kc) * decay[:, None]
    o_inner = jnp.einsum('bhnij,bhnjd->bhnid', inner, vc)
    zeta = jnp.exp((C - 1.0 - i)[None, :] * log_g[:, None])
    xi = jnp.exp((i + 1.0)[None, :] * log_g[:, None])
    kv = jnp.einsum('bhnjd,bhnje->nbhde', kc * zeta[:, None, :, None], vc)
    g_chunk = jnp.exp(C * log_g)[None, :, None, None]

    def step(state, kv_n):
        return g_chunk * state + kv_n, state

    _, s_prev = lax.scan(step, jnp.zeros((B, H, d, d), kv.dtype), kv)
    o_cross = jnp.einsum('bhnid,nbhde->bhnie', qc * xi[:, None, :, None], s_prev)
    o = (o_inner + o_cross).astype(jnp.float32).transpose(0, 2, 3, 1, 4).reshape(B, T, H, d)
    mu = jnp.mean(o, axis=-1, keepdims=True)
    var = jnp.mean(jnp.square(o - mu), axis=-1, keepdims=True)
    o = ((o - mu) * lax.rsqrt(var + LN_EPS)).reshape(B, T, H * d)
    return (o * gn_g.astype(jnp.float32) + gn_b.astype(jnp.float32)).astype(v.dtype)


def _token_mixer(h, w_in, cmp_pos, cmp_w1, cmp_w2, gn_g, gn_b, w_out):
    B, T, _ = h.shape
    proj = h @ w_in
    cuts = np.cumsum(SPLIT_SIZES)[:-1].tolist()
    q_n, kv_n, g_n, qkv_r, g_r = jnp.split(proj, cuts, axis=-1)
    q_nsa = q_n.reshape(B, T, NSA_KV_GROUPS, NSA_REP, NSA_HEAD_DIM).transpose(0, 2, 3, 1, 4)
    kvs = kv_n.reshape(B, T, 2 * N_BRANCH, NSA_KV_GROUPS, NSA_HEAD_DIM).transpose(2, 0, 3, 1, 4)
    k_cmp = _compress(kvs[0], cmp_pos[0], cmp_w1[0], cmp_w2[0])
    v_cmp = _compress(kvs[1], cmp_pos[1], cmp_w1[1], cmp_w2[1])
    gates = jax.nn.sigmoid(g_n.astype(jnp.float32)).reshape(B, T, NSA_KV_GROUPS, NSA_REP, N_BRANCH).transpose(0, 2, 3, 1, 4)
    o_nsa = _nsa(q_nsa, gates, k_cmp, v_cmp, kvs[2], kvs[3], kvs[4], kvs[5])
    q_r, k_r, v_r = [a.reshape(B, T, RET_HEADS, RET_HEAD_DIM) for a in jnp.split(qkv_r, 3, axis=-1)]
    o_ret = jax.nn.silu(g_r) * _retention(q_r, k_r, v_r, gn_g, gn_b)
    return jnp.concatenate([o_nsa, o_ret.astype(o_nsa.dtype)], axis=-1) @ w_out


def setup_inputs(seed: int = 0) -> dict:
    key = jax.random.key(seed)
    ks = jax.random.split(key, 16)
    f32 = jnp.float32
    nrm = lambda k, shape, s: jax.random.normal(k, shape, f32) * s
    return {
        "x": nrm(ks[0], (BATCH, SEQ, D_MODEL), 1.0),
        "p": nrm(ks[1], (DEPTH, BATCH, SEQ, PLE_DIM), 1.0),
        "ffn_w13": nrm(ks[2], (DEPTH, 2, D_MODEL, 2 * D_FF), D_MODEL ** -0.5),
        "ffn_w2": nrm(ks[3], (DEPTH, 2, D_FF, D_MODEL), BETA * D_FF ** -0.5),
        "w_in": nrm(ks[4], (DEPTH, D_MODEL, IN_WIDTH), D_MODEL ** -0.5),
        "cmp_pos": nrm(ks[5], (DEPTH, 2, CMP_BLOCK, NSA_HEAD_DIM), 0.1),
        "cmp_w1": nrm(ks[6], (DEPTH, 2, CMP_BLOCK * NSA_HEAD_DIM, CMP_HIDDEN), (CMP_BLOCK * NSA_HEAD_DIM) ** -0.5),
        "cmp_w2": nrm(ks[7], (DEPTH, 2, CMP_HIDDEN, NSA_HEAD_DIM), CMP_HIDDEN ** -0.5),
        "ret_gn_g": 1.0 + nrm(ks[8], (DEPTH, RET_WIDTH), 0.02),
        "ret_gn_b": nrm(ks[9], (DEPTH, RET_WIDTH), 0.02),
        "w_out": nrm(ks[10], (DEPTH, MIX_WIDTH, D_MODEL), BETA * MIX_WIDTH ** -0.5),
        "w_ple": nrm(ks[11], (DEPTH, PLE_DIM, D_MODEL), BETA * PLE_DIM ** -0.5),
        "w_ple_gate": nrm(ks[12], (DEPTH, D_MODEL, D_MODEL), D_MODEL ** -0.5),
        "ln_g": 1.0 + nrm(ks[13], (DEPTH, N_LN, D_MODEL), 0.02),
        "ln_b": nrm(ks[14], (DEPTH, N_LN, D_MODEL), 0.02),
    }


def reference(x, p, ffn_w13, ffn_w2, w_in, cmp_pos, cmp_w1, cmp_w2, ret_gn_g, ret_gn_b,
              w_out, w_ple, w_ple_gate, ln_g, ln_b):
    for i in range(DEPTH):
        x = _layer_norm(ALPHA * x + 0.5 * _swiglu(x, ffn_w13[i, 0], ffn_w2[i, 0]), ln_g[i, 0], ln_b[i, 0])
        mix = _token_mixer(x, w_in[i], cmp_pos[i], cmp_w1[i], cmp_w2[i], ret_gn_g[i], ret_gn_b[i], w_out[i])
        x = _layer_norm(ALPHA * x + mix, ln_g[i, 1], ln_b[i, 1])
        x = _layer_norm(ALPHA * x + 0.5 * _swiglu(x, ffn_w13[i, 1], ffn_w2[i, 1]), ln_g[i, 2], ln_b[i, 2])
        e = (p[i] @ w_ple[i]) * jax.nn.sigmoid(x @ w_ple_gate[i])
        x = _layer_norm(ALPHA * x + e, ln_g[i, 3], ln_b[i, 3])
    return x
```

```python
import functools
import math

import jax
import jax.numpy as jnp
import numpy as np
from jax import lax
from jax.experimental import pallas as pl
from jax.experimental.pallas import tpu as pltpu

F32 = jnp.float32
BF16 = jnp.bfloat16

D_MODEL = 1024
D_FF = 2816
NSA_GROUPS = 2
NSA_REP = 4
NSA_HEAD_DIM = 64
N_BRANCH = 3
CMP_BLOCK = 32
CMP_STRIDE = 16
CMP_HIDDEN = 256
SLC_BLOCK = 64
SLC_TOPK = 16
WINDOW = 512
Q_BLOCK = 128
RET_HEADS = 4
RET_HEAD_DIM = 128
RET_CHUNK = 128
ROPE_BASE = 10000.0
PLE_DIM = 256
DEPTH = 1
ALPHA = (2.0 * DEPTH) ** 0.25
LN_EPS = 1e-5
NEG = -1e30

NSA_WIDTH = NSA_GROUPS * NSA_REP * NSA_HEAD_DIM
RET_WIDTH = RET_HEADS * RET_HEAD_DIM
GROUP_Q = NSA_REP * NSA_HEAD_DIM
LANES_Q = NSA_REP * Q_BLOCK
GATE_ROWS = 16
TR_Q, TR_VS, TR_VW, TR_G = 0, 512, 640, 768
TR_ROWS = TR_G + NSA_GROUPS * GATE_ROWS
NAT_KC, NAT_VC, NAT_KS, NAT_KW, NAT_RET = 0, 128, 256, 384, 512
NAT_COLS = NAT_RET + 4 * RET_WIDTH

VMEM_LIMIT = 56 * 1024 * 1024
ROW_TILE = 512


def _layer_norm(z, g, b):
    mu = jnp.mean(z, axis=-1, keepdims=True)
    zc = z - mu
    var = jnp.mean(zc * zc, axis=-1, keepdims=True)
    return zc * lax.rsqrt(var + LN_EPS) * g + b


def _sigmoid(v):
    return 1.0 / (1.0 + jnp.exp(-v))


def _params(*sem):
    return pltpu.CompilerParams(dimension_semantics=sem, vmem_limit_bytes=VMEM_LIMIT)


def _resident(shape):
    nd = len(shape)
    return pl.BlockSpec(shape, lambda *_: (0,) * nd, pipeline_mode=pl.Buffered(1))


def _ffn_ln_kernel(x_ref, w1_ref, w3_ref, w2_ref, g_ref, b_ref, o_ref):
    x = x_ref[...]
    xb = x.astype(BF16)
    a = jnp.dot(xb, w1_ref[...], preferred_element_type=F32)
    u = jnp.dot(xb, w3_ref[...], preferred_element_type=F32)
    h = (a * _sigmoid(a) * u).astype(BF16)
    y = jnp.dot(h, w2_ref[...], preferred_element_type=F32)
    o_ref[...] = _layer_norm(ALPHA * x + 0.5 * y, g_ref[...], b_ref[...])


def _ffn_ln(x2d, w13, w2, g, b):
    n = x2d.shape[0]
    w1 = w13[:, :D_FF].astype(BF16)
    w3 = w13[:, D_FF:].astype(BF16)
    row = pl.BlockSpec((ROW_TILE, D_MODEL), lambda i: (i, 0))
    return pl.pallas_call(
        _ffn_ln_kernel,
        out_shape=jax.ShapeDtypeStruct((n, D_MODEL), F32),
        grid=(n // ROW_TILE,),
        in_specs=[row, _resident((D_MODEL, D_FF)), _resident((D_MODEL, D_FF)),
                  _resident((D_FF, D_MODEL)), _resident((1, D_MODEL)), _resident((1, D_MODEL))],
        out_specs=row,
        compiler_params=_params("parallel"),
        name="ffn_ln",
    )(x2d, w1, w3, w2.astype(BF16), g.reshape(1, -1), b.reshape(1, -1))


def _in_proj_kernel(x_ref, wn_ref, wt_ref, nat_ref, tr_ref):
    xb = x_ref[0].astype(BF16)
    nat_ref[0] = jnp.dot(xb, wn_ref[...], preferred_element_type=F32)
    tr_ref[0] = lax.dot_general(wt_ref[...], xb, (((1,), (1,)), ((), ())),
                                preferred_element_type=F32)


def _split_w_in(w_in):
    c_q = NSA_WIDTH
    c_kv = c_q + 2 * N_BRANCH * NSA_GROUPS * NSA_HEAD_DIM
    c_g = c_kv + NSA_GROUPS * NSA_REP * N_BRANCH
    kv = w_in[:, c_q:c_kv].reshape(D_MODEL, 2 * N_BRANCH, NSA_GROUPS * NSA_HEAD_DIM)
    gates = w_in[:, c_kv:c_g].reshape(D_MODEL, NSA_GROUPS, NSA_REP * N_BRANCH)
    gates = jnp.pad(gates, ((0, 0), (0, 0), (0, GATE_ROWS - NSA_REP * N_BRANCH)))
    w_nat = jnp.concatenate([kv[:, 0], kv[:, 1], kv[:, 2], kv[:, 4], w_in[:, c_g:]], axis=1)
    w_tr = jnp.concatenate([w_in[:, :c_q], kv[:, 3], kv[:, 5],
                            gates.reshape(D_MODEL, NSA_GROUPS * GATE_ROWS)], axis=1)
    return w_nat.astype(BF16), w_tr.T.astype(BF16)


def _in_proj(x, w_in):
    bsz, t, _ = x.shape
    w_nat, w_trt = _split_w_in(w_in)
    return pl.pallas_call(
        _in_proj_kernel,
        out_shape=(jax.ShapeDtypeStruct((bsz, t, NAT_COLS), F32),
                   jax.ShapeDtypeStruct((bsz, TR_ROWS, t), F32)),
        grid=(bsz, t // ROW_TILE),
        in_specs=[pl.BlockSpec((1, ROW_TILE, D_MODEL), lambda b, i: (b, i, 0)),
                  _resident((D_MODEL, NAT_COLS)), _resident((TR_ROWS, D_MODEL))],
        out_specs=(pl.BlockSpec((1, ROW_TILE, NAT_COLS), lambda b, i: (b, i, 0)),
                   pl.BlockSpec((1, TR_ROWS, ROW_TILE), lambda b, i: (b, 0, i))),
        compiler_params=_params("parallel", "parallel"),
        name="in_proj",
    )(x, w_nat, w_trt)


def _gelu_tanh(v):
    return 0.5 * v * (1.0 + jnp.tanh(math.sqrt(2.0 / math.pi) * (v + 0.044715 * (v * v * v))))


def _compress_hidden(h, pos_ref, w1_ref):
    top = jnp.dot((h + pos_ref[0:1, :]).astype(BF16), w1_ref[0], preferred_element_type=F32)
    bot = jnp.dot((h + pos_ref[1:2, :]).astype(BF16), w1_ref[1], preferred_element_type=F32)
    nc = h.shape[0]
    return _gelu_tanh(top + pltpu.roll(bot, nc - 1, 0)).astype(BF16)


def _compress_kernel(hk_ref, hv_ref, pk_ref, pv_ref, w1k_ref, w1v_ref, w2k_ref, w2vt_ref,
                     kc_ref, vct_ref):
    nc = hk_ref.shape[2]
    gk = _compress_hidden(hk_ref[0, 0], pk_ref, w1k_ref)
    kc = jnp.dot(gk, w2k_ref[...], preferred_element_type=F32)
    row = lax.broadcasted_iota(jnp.int32, kc.shape, 0)
    kc_ref[0, 0] = jnp.where(row < nc - 1, kc, 0.0)
    gv = _compress_hidden(hv_ref[0, 0], pv_ref, w1v_ref)
    vct = lax.dot_general(w2vt_ref[...], gv, (((1,), (1,)), ((), ())),
                          preferred_element_type=F32)
    col = lax.broadcasted_iota(jnp.int32, vct.shape, 1)
    vct_ref[0, 0] = jnp.where(col < nc - 1, vct, 0.0)


def _compress(hk, hv, cmp_pos, cmp_w1, cmp_w2):
    bsz, g, nc, hw = hk.shape
    dh = NSA_HEAD_DIM
    half = CMP_BLOCK // 2
    pos = cmp_pos.reshape(2, 2, half * dh)
    w1 = cmp_w1.reshape(2, 2, half * dh, CMP_HIDDEN).astype(BF16)
    blk = pl.BlockSpec((1, 1, nc, hw), lambda b, j: (b, j, 0, 0))
    return pl.pallas_call(
        _compress_kernel,
        out_shape=(jax.ShapeDtypeStruct((bsz, g, nc, dh), F32),
                   jax.ShapeDtypeStruct((bsz, g, dh, nc), F32)),
        grid=(bsz, g),
        in_specs=[blk, blk, _resident((2, hw)), _resident((2, hw)),
                  _resident((2, hw, CMP_HIDDEN)), _resident((2, hw, CMP_HIDDEN)),
                  _resident((CMP_HIDDEN, dh)), _resident((dh, CMP_HIDDEN))],
        out_specs=(pl.BlockSpec((1, 1, nc, dh), lambda b, j: (b, j, 0, 0)),
                   pl.BlockSpec((1, 1, dh, nc), lambda b, j: (b, j, 0, 0))),
        compiler_params=_params("parallel", "parallel"),
        name="compress",
    )(hk, hv, pos[0], pos[1], w1[0], w1[1], cmp_w2[0].astype(BF16), cmp_w2[1].T.astype(BF16))


def _q_lanes(qt_ref):
    qt = qt_ref[0]
    parts = [qt[r * NSA_HEAD_DIM:(r + 1) * NSA_HEAD_DIM, :] for r in range(NSA_REP)]
    return (jnp.concatenate(parts, axis=1) * (NSA_HEAD_DIM ** -0.5)).astype(BF16)


def _lane_token(shape, q0):
    return q0 + (lax.broadcasted_iota(jnp.int32, shape, 1) & (Q_BLOCK - 1))


def _nsa_select_kernel(qt_ref, kc_ref, vct_ref, ovl_ref, oct_ref, nsel_ref):
    q0 = pl.program_id(2) * Q_BLOCK
    qm = _q_lanes(qt_ref)
    s = jnp.dot(kc_ref[0, 0].astype(BF16), qm, preferred_element_type=F32)
    c_last = lax.broadcasted_iota(jnp.int32, s.shape, 0) * CMP_STRIDE + (CMP_BLOCK - 1)
    mask = c_last <= _lane_token(s.shape, q0)
    s = jnp.where(mask, s, NEG)
    e = jnp.exp(s - jnp.max(s, axis=0, keepdims=True))
    p = jnp.where(mask, e / jnp.sum(e, axis=0, keepdims=True), 0.0)
    o = jnp.dot(vct_ref[0, 0].astype(BF16), p.astype(BF16), preferred_element_type=F32)
    for r in range(NSA_REP):
        oct_ref[0, r] = o[:, r * Q_BLOCK:(r + 1) * Q_BLOCK]
    psum = p[:, 0:Q_BLOCK]
    for r in range(1, NSA_REP):
        psum = psum + p[:, r * Q_BLOCK:(r + 1) * Q_BLOCK]
    p_hi = psum.astype(BF16)
    p_lo = (psum - p_hi.astype(F32)).astype(BF16)
    ovl = ovl_ref[...]
    imp = (jnp.dot(ovl, p_hi, preferred_element_type=F32)
           + jnp.dot(ovl, p_lo, preferred_element_type=F32))
    blk = lax.broadcasted_iota(jnp.int32, imp.shape, 0)
    cur = _lane_token(imp.shape, q0) // SLC_BLOCK
    valid = blk <= cur
    forced = (blk == 0) | (blk >= cur - 1)
    val = jnp.where(valid, jnp.where(forced, jnp.inf, imp), -jnp.inf)
    blk_f = blk.astype(F32)
    n_blk = imp.shape[0]
    sel = jnp.zeros(imp.shape, F32)
    for _ in range(min(SLC_TOPK, n_blk)):
        top = jnp.max(val, axis=0, keepdims=True)
        first = jnp.min(jnp.where(val == top, blk_f, float(n_blk)), axis=0, keepdims=True)
        pick = blk_f == first
        sel = jnp.where(pick, 1.0, sel)
        val = jnp.where(pick, -jnp.inf, val)
    nsel_ref[0, 0] = jnp.where(valid, 1.0 - sel, 1.0).astype(BF16)


def _overlap_t(t):
    n_cmp = t // CMP_STRIDE
    n_slc = t // SLC_BLOCK
    cs = np.arange(n_cmp) * CMP_STRIDE
    ss = np.arange(n_slc) * SLC_BLOCK
    ov = np.clip(np.minimum(cs[None, :] + CMP_BLOCK, ss[:, None] + SLC_BLOCK)
                 - np.maximum(cs[None, :], ss[:, None]), 0, None) / CMP_BLOCK
    return jnp.asarray(ov, dtype=BF16)


def _nsa_select(tr, kc, vct):
    bsz, _, t = tr.shape
    g, nc, dh = kc.shape[1:]
    ns = t // SLC_BLOCK
    nq = t // Q_BLOCK
    return pl.pallas_call(
        _nsa_select_kernel,
        out_shape=(jax.ShapeDtypeStruct((bsz, g * NSA_REP, dh, t), F32),
                   jax.ShapeDtypeStruct((bsz, g, ns, t), BF16)),
        grid=(bsz, g, nq),
        in_specs=[pl.BlockSpec((1, GROUP_Q, Q_BLOCK), lambda b, j, i: (b, j, i)),
                  pl.BlockSpec((1, 1, nc, dh), lambda b, j, i: (b, j, 0, 0)),
                  pl.BlockSpec((1, 1, dh, nc), lambda b, j, i: (b, j, 0, 0)),
                  _resident((ns, nc))],
        out_specs=(pl.BlockSpec((1, NSA_REP, dh, Q_BLOCK), lambda b, j, i: (b, j, 0, i)),
                   pl.BlockSpec((1, 1, ns, Q_BLOCK), lambda b, j, i: (b, j, 0, i))),
        compiler_params=_params("parallel", "parallel", "parallel"),
        name="nsa_select",
    )(tr, kc, vct, _overlap_t(t))


def _nsa_main_kernel(qt_ref, nsel_ref, ka_ref, vst_ref, kw_ref, vwt_ref, oct_ref, gt_ref,
                     o_ref, *, key_tile):
    qb = pl.program_id(2)
    q0 = qb * Q_BLOCK
    dh = NSA_HEAD_DIM
    qm = _q_lanes(qt_ref)
    nsel = nsel_ref[0, 0]
    q_aug = jnp.concatenate([qm, jnp.zeros((Q_BLOCK - dh, LANES_Q), BF16),
                             jnp.concatenate([nsel] * NSA_REP, axis=1)], axis=0)
    tok = _lane_token((key_tile, LANES_Q), q0)
    kpos0 = lax.broadcasted_iota(jnp.int32, (key_tile, LANES_Q), 0)

    def sel_step(kt, carry):
        m, l, acc = carry
        k0 = pl.multiple_of(kt * key_tile, key_tile)
        s = jnp.dot(ka_ref[0, 0, pl.ds(k0, key_tile), :], q_aug, preferred_element_type=F32)
        s = jnp.where(kpos0 + k0 <= tok, s, NEG)
        m_new = jnp.maximum(m, jnp.max(s, axis=0, keepdims=True))
        a = jnp.exp(m - m_new)
        p = jnp.exp(s - m_new)
        l = l * a + jnp.sum(p, axis=0, keepdims=True)
        v = vst_ref[0, :, pl.ds(k0, key_tile)].astype(BF16)
        acc = acc * a + jnp.dot(v, p.astype(BF16), preferred_element_type=F32)
        return m_new, l, acc

    n_tiles = (q0 + Q_BLOCK + key_tile - 1) // key_tile
    init = (jnp.full((1, LANES_Q), NEG, F32), jnp.zeros((1, LANES_Q), F32), jnp.zeros((dh, LANES_Q), F32))
    _, l_s, acc_s = lax.fori_loop(0, n_tiles, sel_step, init)
    o_s = acc_s / l_s

    wlen = WINDOW + Q_BLOCK
    w0 = pl.multiple_of(jnp.maximum(q0 - WINDOW, 0), Q_BLOCK)
    s = jnp.dot(kw_ref[0, 0, pl.ds(w0, wlen), :], qm, preferred_element_type=F32)
    dist = _lane_token(s.shape, q0) - (lax.broadcasted_iota(jnp.int32, s.shape, 0) + w0)
    s = jnp.where((dist >= 0) & (dist < WINDOW), s, NEG)
    e = jnp.exp(s - jnp.max(s, axis=0, keepdims=True))
    p = e / jnp.sum(e, axis=0, keepdims=True)
    o_w = jnp.dot(vwt_ref[0, :, pl.ds(w0, wlen)].astype(BF16), p.astype(BF16),
                  preferred_element_type=F32)

    gates = _sigmoid(gt_ref[0])
    outs = []
    for r in range(NSA_REP):
        sl = slice(r * Q_BLOCK, (r + 1) * Q_BLOCK)
        g_c, g_s, g_w = (gates[r * N_BRANCH + i:r * N_BRANCH + i + 1, :] for i in range(N_BRANCH))
        outs.append(g_c * oct_ref[0, r] + g_s * o_s[:, sl] + g_w * o_w[:, sl])
    o_ref[0] = jnp.concatenate(outs, axis=0).T


def _nsa_main(tr, nsel, k_aug, kw, oct_, key_tile):
    bsz, _, t = tr.shape
    g = k_aug.shape[1]
    dh = NSA_HEAD_DIM
    ns = t // SLC_BLOCK
    nq = t // Q_BLOCK
    aw = k_aug.shape[-1]
    return pl.pallas_call(
        functools.partial(_nsa_main_kernel, key_tile=key_tile),
        out_shape=jax.ShapeDtypeStruct((bsz, t, NSA_WIDTH), F32),
        grid=(bsz, g, nq),
        in_specs=[pl.BlockSpec((1, GROUP_Q, Q_BLOCK), lambda b, j, i: (b, j, i)),
                  pl.BlockSpec((1, 1, ns, Q_BLOCK), lambda b, j, i: (b, j, 0, i)),
                  pl.BlockSpec((1, 1, t, aw), lambda b, j, i: (b, j, 0, 0)),
                  pl.BlockSpec((1, dh, t), lambda b, j, i: (b, TR_VS // dh + j, 0)),
                  pl.BlockSpec((1, 1, t, dh), lambda b, j, i: (b, j, 0, 0)),
                  pl.BlockSpec((1, dh, t), lambda b, j, i: (b, TR_VW // dh + j, 0)),
                  pl.BlockSpec((1, NSA_REP, dh, Q_BLOCK), lambda b, j, i: (b, j, 0, i)),
                  pl.BlockSpec((1, GATE_ROWS, Q_BLOCK), lambda b, j, i: (b, TR_G // GATE_ROWS + j, i))],
        out_specs=pl.BlockSpec((1, Q_BLOCK, GROUP_Q), lambda b, j, i: (b, i, j)),
        compiler_params=_params("parallel", "parallel", "arbitrary"),
        name="nsa_main",
    )(tr, nsel, k_aug, tr, kw, tr, oct_, tr)


def _augment_keys(k_slc):
    bsz, g, t, dh = k_slc.shape
    ns = t // SLC_BLOCK
    own = (np.arange(t)[:, None] // SLC_BLOCK) == np.arange(ns)[None, :]
    bias = jnp.asarray(np.where(own, NEG, 0.0), dtype=BF16)
    return jnp.concatenate([k_slc.astype(BF16), jnp.zeros((bsz, g, t, Q_BLOCK - dh), BF16),
                            jnp.broadcast_to(bias, (bsz, g, t, ns))], axis=-1)


def _retention_kernel(q_ref, k_ref, v_ref, gate_ref, cos_ref, sin_ref, gng_ref, gnb_ref,
                      o_ref, state_ref):
    c = RET_CHUNK
    d = RET_HEAD_DIM

    @pl.when(pl.program_id(1) == 0)
    def _():
        state_ref[...] = jnp.zeros_like(state_ref)

    cos = cos_ref[...]
    sin = sin_ref[...]
    i_row = lax.broadcasted_iota(jnp.int32, (c, c), 0)
    i_col = lax.broadcasted_iota(jnp.int32, (c, c), 1)
    rel = (i_row - i_col).astype(F32)
    pos = lax.broadcasted_iota(jnp.int32, (c, 1), 0).astype(F32)
    for h in range(RET_HEADS):
        log_g = math.log(1.0 - 2.0 ** (-5.0 - h))
        sl = slice(h * d, (h + 1) * d)
        q = q_ref[0, :, sl]
        k = k_ref[0, :, sl]
        q = q * cos + pltpu.roll(q, d // 2, 1) * sin
        k = (k * cos + pltpu.roll(k, d // 2, 1) * sin) * (d ** -0.5)
        vb = v_ref[0, :, sl].astype(BF16)
        decay = jnp.where(rel >= 0, jnp.exp(jnp.maximum(rel, 0.0) * log_g), 0.0)
        inner = lax.dot_general(q.astype(BF16), k.astype(BF16), (((1,), (1,)), ((), ())),
                                preferred_element_type=F32) * decay
        o = jnp.dot(inner.astype(BF16), vb, preferred_element_type=F32)
        state = state_ref[h]
        xi = jnp.exp((pos + 1.0) * log_g)
        o = o + jnp.dot((q * xi).astype(BF16), state.astype(BF16), preferred_element_type=F32)
        zeta = jnp.exp((c - 1.0 - pos) * log_g)
        kv = jnp.dot((k * zeta).T.astype(BF16), vb, preferred_element_type=F32)
        state_ref[h] = math.exp(c * log_g) * state + kv
        mu = jnp.mean(o, axis=-1, keepdims=True)
        oc = o - mu
        var = jnp.mean(oc * oc, axis=-1, keepdims=True)
        y = oc * lax.rsqrt(var + LN_EPS) * gng_ref[:, sl] + gnb_ref[:, sl]
        gate = gate_ref[0, :, sl]
        o_ref[0, :, sl] = gate * _sigmoid(gate) * y


def _rotary_tables(t):
    d = RET_HEAD_DIM
    freqs = ROPE_BASE ** (-jnp.arange(0, d, 2, dtype=F32) / d)
    ang = jnp.arange(t, dtype=F32)[:, None] * freqs[None, :]
    cos, sin = jnp.cos(ang), jnp.sin(ang)
    return jnp.concatenate([cos, cos], axis=1), jnp.concatenate([-sin, sin], axis=1)


def _retention(nat, gn_g, gn_b):
    bsz, t, _ = nat.shape
    c = RET_CHUNK
    cos2, sin2 = _rotary_tables(t)
    col0 = NAT_RET // RET_WIDTH
    part = lambda j: pl.BlockSpec((1, c, RET_WIDTH), lambda b, n: (b, n, col0 + j))
    tab = pl.BlockSpec((c, RET_HEAD_DIM), lambda b, n: (n, 0))
    return pl.pallas_call(
        _retention_kernel,
        out_shape=jax.ShapeDtypeStruct((bsz, t, RET_WIDTH), F32),
        grid=(bsz, t // c),
        in_specs=[part(0), part(1), part(2), part(3), tab, tab,
                  _resident((1, RET_WIDTH)), _resident((1, RET_WIDTH))],
        out_specs=pl.BlockSpec((1, c, RET_WIDTH), lambda b, n: (b, n, 0)),
        scratch_shapes=[pltpu.VMEM((RET_HEADS, RET_HEAD_DIM, RET_HEAD_DIM), F32)],
        compiler_params=_params("parallel", "arbitrary"),
        name="retention",
    )(nat, nat, nat, nat, cos2, sin2, gn_g.reshape(1, -1), gn_b.reshape(1, -1))


def _out_ln_kernel(x_ref, on_ref, or_ref, wn_ref, wr_ref, g_ref, b_ref, o_ref):
    mix = (jnp.dot(on_ref[...].astype(BF16), wn_ref[...], preferred_element_type=F32)
           + jnp.dot(or_ref[...].astype(BF16), wr_ref[...], preferred_element_type=F32))
    o_ref[...] = _layer_norm(ALPHA * x_ref[...] + mix, g_ref[...], b_ref[...])


def _out_ln(x2d, o_nsa, o_ret, w_out, g, b):
    n = x2d.shape[0]
    row = lambda w: pl.BlockSpec((ROW_TILE, w), lambda i: (i, 0))
    return pl.pallas_call(
        _out_ln_kernel,
        out_shape=jax.ShapeDtypeStruct((n, D_MODEL), F32),
        grid=(n // ROW_TILE,),
        in_specs=[row(D_MODEL), row(NSA_WIDTH), row(RET_WIDTH),
                  _resident((NSA_WIDTH, D_MODEL)), _resident((RET_WIDTH, D_MODEL)),
                  _resident((1, D_MODEL)), _resident((1, D_MODEL))],
        out_specs=row(D_MODEL),
        compiler_params=_params("parallel"),
        name="out_ln",
    )(x2d, o_nsa, o_ret, w_out[:NSA_WIDTH].astype(BF16), w_out[NSA_WIDTH:].astype(BF16),
      g.reshape(1, -1), b.reshape(1, -1))


def _ple_ln_kernel(x_ref, p_ref, wp_ref, wg_ref, g_ref, b_ref, o_ref):
    x = x_ref[...]
    e = (jnp.dot(p_ref[...].astype(BF16), wp_ref[...], preferred_element_type=F32)
         * _sigmoid(jnp.dot(x.astype(BF16), wg_ref[...], preferred_element_type=F32)))
    o_ref[...] = _layer_norm(ALPHA * x + e, g_ref[...], b_ref[...])


def _ple_ln(x2d, p2d, w_ple, w_gate, g, b):
    n = x2d.shape[0]
    row = lambda w: pl.BlockSpec((ROW_TILE, w), lambda i: (i, 0))
    return pl.pallas_call(
        _ple_ln_kernel,
        out_shape=jax.ShapeDtypeStruct((n, D_MODEL), F32),
        grid=(n // ROW_TILE,),
        in_specs=[row(D_MODEL), row(PLE_DIM), _resident((PLE_DIM, D_MODEL)),
                  _resident((D_MODEL, D_MODEL)), _resident((1, D_MODEL)), _resident((1, D_MODEL))],
        out_specs=row(D_MODEL),
        compiler_params=_params("parallel"),
        name="ple_ln",
    )(x2d, p2d, w_ple.astype(BF16), w_gate.astype(BF16), g.reshape(1, -1), b.reshape(1, -1))


def _token_mixer(x, w_in, cmp_pos, cmp_w1, cmp_w2, gn_g, gn_b):
    bsz, t, _ = x.shape
    g, dh = NSA_GROUPS, NSA_HEAD_DIM
    nat, tr = _in_proj(x, w_in)
    by_group = lambda c0: nat[:, :, c0:c0 + g * dh].reshape(bsz, t, g, dh).transpose(0, 2, 1, 3)
    half_blocks = lambda a: a.reshape(bsz, g, t // CMP_STRIDE, CMP_STRIDE * dh)
    kc, vct = _compress(half_blocks(by_group(NAT_KC)), half_blocks(by_group(NAT_VC)),
                        cmp_pos, cmp_w1, cmp_w2)
    oct_, nsel = _nsa_select(tr, kc, vct)
    key_tile = min(512, t)
    o_nsa = _nsa_main(tr, nsel, _augment_keys(by_group(NAT_KS)), by_group(NAT_KW).astype(BF16),
                      oct_, key_tile)
    o_ret = _retention(nat, gn_g, gn_b)
    return o_nsa, o_ret


def kernel(x, p, ffn_w13, ffn_w2, w_in, cmp_pos, cmp_w1, cmp_w2, ret_gn_g, ret_gn_b,
           w_out, w_ple, w_ple_gate, ln_g, ln_b):
    bsz, t, d = x.shape
    n = bsz * t
    for i in range(ffn_w13.shape[0]):
        h = _ffn_ln(x.reshape(n, d), ffn_w13[i, 0], ffn_w2[i, 0], ln_g[i, 0], ln_b[i, 0])
        o_nsa, o_ret = _token_mixer(h.reshape(bsz, t, d), w_in[i], cmp_pos[i], cmp_w1[i], cmp_w2[i],
                                    ret_gn_g[i], ret_gn_b[i])
        h = _out_ln(h, o_nsa.reshape(n, -1), o_ret.reshape(n, -1), w_out[i], ln_g[i, 1], ln_b[i, 1])
        h = _ffn_ln(h, ffn_w13[i, 1], ffn_w2[i, 1], ln_g[i, 2], ln_b[i, 2])
        h = _ple_ln(h, p[i].reshape(n, -1), w_ple[i], w_ple_gate[i], ln_g[i, 3], ln_b[i, 3])
        x = h.reshape(bsz, t, d)
    return x
```

```python
import functools
import math

import jax
import jax.numpy as jnp
import numpy as np
from jax import lax
from jax.experimental import pallas as pl
from jax.experimental.pallas import tpu as pltpu

F32 = jnp.float32
BF16 = jnp.bfloat16

D_MODEL = 1024
D_FF = 2816
NSA_GROUPS = 2
NSA_REP = 4
NSA_HEAD_DIM = 64
N_BRANCH = 3
CMP_BLOCK = 32
CMP_STRIDE = 16
CMP_HIDDEN = 256
SLC_BLOCK = 64
SLC_TOPK = 16
WINDOW = 512
Q_BLOCK = 128
RET_HEADS = 4
RET_HEAD_DIM = 128
RET_CHUNK = 128
ROPE_BASE = 10000.0
PLE_DIM = 256
DEPTH = 1
ALPHA = (2.0 * DEPTH) ** 0.25
LN_EPS = 1e-5
NEG = -1e30
LOG2_E = math.log2(math.e)

NSA_WIDTH = NSA_GROUPS * NSA_REP * NSA_HEAD_DIM
RET_WIDTH = RET_HEADS * RET_HEAD_DIM
GROUP_Q = NSA_REP * NSA_HEAD_DIM
LANES_Q = NSA_REP * Q_BLOCK
GATE_ROWS = 16
TR_Q, TR_VS, TR_VW, TR_G = 0, 512, 640, 768
TR_ROWS = TR_G + NSA_GROUPS * GATE_ROWS
NAT_KC, NAT_VC, NAT_KS, NAT_KW, NAT_RET = 0, 128, 256, 384, 512
NAT_COLS = NAT_RET + 4 * RET_WIDTH

KEY_CHUNKS = 1
VMEM_LIMIT = 56 * 1024 * 1024
ROW_TILE = 512


def _layer_norm(z, g, b):
    mu = jnp.mean(z, axis=-1, keepdims=True)
    zc = z - mu
    var = jnp.mean(zc * zc, axis=-1, keepdims=True)
    return zc * lax.rsqrt(var + LN_EPS) * g + b


def _sigmoid(v):
    return 1.0 / (1.0 + jnp.exp(-v))


def _params(*sem):
    return pltpu.CompilerParams(dimension_semantics=sem, vmem_limit_bytes=VMEM_LIMIT)


def _resident(shape):
    nd = len(shape)
    return pl.BlockSpec(shape, lambda *_: (0,) * nd, pipeline_mode=pl.Buffered(1))


def _ffn_ln_kernel(x_ref, w1_ref, w3_ref, w2_ref, g_ref, b_ref, o_ref):
    x = x_ref[...]
    xb = x.astype(BF16)
    a = jnp.dot(xb, w1_ref[...], preferred_element_type=F32)
    u = jnp.dot(xb, w3_ref[...], preferred_element_type=F32)
    h = (a * _sigmoid(a) * u).astype(BF16)
    y = jnp.dot(h, w2_ref[...], preferred_element_type=F32)
    o_ref[...] = _layer_norm(ALPHA * x + 0.5 * y, g_ref[...], b_ref[...])


def _ffn_ln(x2d, w13, w2, g, b):
    n = x2d.shape[0]
    w1 = w13[:, :D_FF].astype(BF16)
    w3 = w13[:, D_FF:].astype(BF16)
    row = pl.BlockSpec((ROW_TILE, D_MODEL), lambda i: (i, 0))
    return pl.pallas_call(
        _ffn_ln_kernel,
        out_shape=jax.ShapeDtypeStruct((n, D_MODEL), F32),
        grid=(n // ROW_TILE,),
        in_specs=[row, _resident((D_MODEL, D_FF)), _resident((D_MODEL, D_FF)),
                  _resident((D_FF, D_MODEL)), _resident((1, D_MODEL)), _resident((1, D_MODEL))],
        out_specs=row,
        compiler_params=_params("parallel"),
        name="ffn_ln",
    )(x2d, w1, w3, w2.astype(BF16), g.reshape(1, -1), b.reshape(1, -1))


def _in_proj_kernel(x_ref, wn_ref, wt_ref, nat_ref, tr_ref):
    xb = x_ref[0].astype(BF16)
    nat_ref[0] = jnp.dot(xb, wn_ref[...], preferred_element_type=F32)
    tr_ref[0] = lax.dot_general(wt_ref[...], xb, (((1,), (1,)), ((), ())),
                                preferred_element_type=F32)


def _split_w_in(w_in):
    c_q = NSA_WIDTH
    c_kv = c_q + 2 * N_BRANCH * NSA_GROUPS * NSA_HEAD_DIM
    c_g = c_kv + NSA_GROUPS * NSA_REP * N_BRANCH
    kv = w_in[:, c_q:c_kv].reshape(D_MODEL, 2 * N_BRANCH, NSA_GROUPS * NSA_HEAD_DIM)
    gates = w_in[:, c_kv:c_g].reshape(D_MODEL, NSA_GROUPS, NSA_REP * N_BRANCH)
    gates = jnp.pad(gates, ((0, 0), (0, 0), (0, GATE_ROWS - NSA_REP * N_BRANCH)))
    w_nat = jnp.concatenate([kv[:, 0], kv[:, 1], kv[:, 2], kv[:, 4], w_in[:, c_g:]], axis=1)
    w_tr = jnp.concatenate([w_in[:, :c_q], kv[:, 3], kv[:, 5],
                            gates.reshape(D_MODEL, NSA_GROUPS * GATE_ROWS)], axis=1)
    return w_nat.astype(BF16), w_tr.T.astype(BF16)


def _in_proj(x, w_in):
    bsz, t, _ = x.shape
    w_nat, w_trt = _split_w_in(w_in)
    return pl.pallas_call(
        _in_proj_kernel,
        out_shape=(jax.ShapeDtypeStruct((bsz, t, NAT_COLS), F32),
                   jax.ShapeDtypeStruct((bsz, TR_ROWS, t), F32)),
        grid=(bsz, t // ROW_TILE),
        in_specs=[pl.BlockSpec((1, ROW_TILE, D_MODEL), lambda b, i: (b, i, 0)),
                  _resident((D_MODEL, NAT_COLS)), _resident((TR_ROWS, D_MODEL))],
        out_specs=(pl.BlockSpec((1, ROW_TILE, NAT_COLS), lambda b, i: (b, i, 0)),
                   pl.BlockSpec((1, TR_ROWS, ROW_TILE), lambda b, i: (b, 0, i))),
        compiler_params=_params("parallel", "parallel"),
        name="in_proj",
    )(x, w_nat, w_trt)


def _gelu_tanh(v):
    return 0.5 * v * (1.0 + jnp.tanh(math.sqrt(2.0 / math.pi) * (v + 0.044715 * (v * v * v))))


def _compress_hidden(h, pos_ref, w1_ref):
    top = jnp.dot((h + pos_ref[0:1, :]).astype(BF16), w1_ref[0], preferred_element_type=F32)
    bot = jnp.dot((h + pos_ref[1:2, :]).astype(BF16), w1_ref[1], preferred_element_type=F32)
    nc = h.shape[0]
    return _gelu_tanh(top + pltpu.roll(bot, nc - 1, 0)).astype(BF16)


def _compress_kernel(hk_ref, hv_ref, pk_ref, pv_ref, w1k_ref, w1v_ref, w2k_ref, w2vt_ref,
                     kc_ref, vct_ref):
    nc = hk_ref.shape[2]
    gk = _compress_hidden(hk_ref[0, 0], pk_ref, w1k_ref)
    kc = jnp.dot(gk, w2k_ref[...], preferred_element_type=F32)
    row = lax.broadcasted_iota(jnp.int32, kc.shape, 0)
    kc_ref[0, 0] = jnp.where(row < nc - 1, kc, 0.0)
    gv = _compress_hidden(hv_ref[0, 0], pv_ref, w1v_ref)
    vct = lax.dot_general(w2vt_ref[...], gv, (((1,), (1,)), ((), ())),
                          preferred_element_type=F32)
    col = lax.broadcasted_iota(jnp.int32, vct.shape, 1)
    vct_ref[0, 0] = jnp.where(col < nc - 1, vct, 0.0)


def _compress(hk, hv, cmp_pos, cmp_w1, cmp_w2):
    bsz, g, nc, hw = hk.shape
    dh = NSA_HEAD_DIM
    half = CMP_BLOCK // 2
    pos = cmp_pos.reshape(2, 2, half * dh)
    w1 = cmp_w1.reshape(2, 2, half * dh, CMP_HIDDEN).astype(BF16)
    blk = pl.BlockSpec((1, 1, nc, hw), lambda b, j: (b, j, 0, 0))
    return pl.pallas_call(
        _compress_kernel,
        out_shape=(jax.ShapeDtypeStruct((bsz, g, nc, dh), F32),
                   jax.ShapeDtypeStruct((bsz, g, dh, nc), F32)),
        grid=(bsz, g),
        in_specs=[blk, blk, _resident((2, hw)), _resident((2, hw)),
                  _resident((2, hw, CMP_HIDDEN)), _resident((2, hw, CMP_HIDDEN)),
                  _resident((CMP_HIDDEN, dh)), _resident((dh, CMP_HIDDEN))],
        out_specs=(pl.BlockSpec((1, 1, nc, dh), lambda b, j: (b, j, 0, 0)),
                   pl.BlockSpec((1, 1, dh, nc), lambda b, j: (b, j, 0, 0))),
        compiler_params=_params("parallel", "parallel"),
        name="compress",
    )(hk, hv, pos[0], pos[1], w1[0], w1[1], cmp_w2[0].astype(BF16), cmp_w2[1].T.astype(BF16))


def _q_lanes(qt_ref):
    qt = qt_ref[0]
    parts = [qt[r * NSA_HEAD_DIM:(r + 1) * NSA_HEAD_DIM, :] for r in range(NSA_REP)]
    return (jnp.concatenate(parts, axis=1) * (NSA_HEAD_DIM ** -0.5 * LOG2_E)).astype(BF16)


def _lane_token(shape, q0):
    return q0 + (lax.broadcasted_iota(jnp.int32, shape, 1) & (Q_BLOCK - 1))


def _nsa_select_kernel(qt_ref, kc_ref, vct_ref, ovl_ref, oct_ref, nsel_ref):
    q0 = pl.program_id(2) * Q_BLOCK
    qm = _q_lanes(qt_ref)
    s = jnp.dot(kc_ref[0, 0].astype(BF16), qm, preferred_element_type=F32)
    c_last = lax.broadcasted_iota(jnp.int32, s.shape, 0) * CMP_STRIDE + (CMP_BLOCK - 1)
    mask = c_last <= _lane_token(s.shape, q0)
    s = jnp.where(mask, s, NEG)
    e = jnp.exp2(s - jnp.max(s, axis=0, keepdims=True))
    p = jnp.where(mask, e / jnp.sum(e, axis=0, keepdims=True), 0.0)
    o = jnp.dot(vct_ref[0, 0].astype(BF16), p.astype(BF16), preferred_element_type=F32)
    for r in range(NSA_REP):
        oct_ref[0, r] = o[:, r * Q_BLOCK:(r + 1) * Q_BLOCK]
    psum = p[:, 0:Q_BLOCK]
    for r in range(1, NSA_REP):
        psum = psum + p[:, r * Q_BLOCK:(r + 1) * Q_BLOCK]
    p_hi = psum.astype(BF16)
    p_lo = (psum - p_hi.astype(F32)).astype(BF16)
    ovl = ovl_ref[...]
    imp = (jnp.dot(ovl, p_hi, preferred_element_type=F32)
           + jnp.dot(ovl, p_lo, preferred_element_type=F32))
    blk = lax.broadcasted_iota(jnp.int32, imp.shape, 0)
    cur = _lane_token(imp.shape, q0) // SLC_BLOCK
    valid = blk <= cur
    forced = (blk == 0) | (blk >= cur - 1)
    val = jnp.where(valid, jnp.where(forced, jnp.inf, imp), -jnp.inf)
    blk_f = blk.astype(F32)
    n_blk = imp.shape[0]
    sel = jnp.zeros(imp.shape, F32)
    for _ in range(min(SLC_TOPK, n_blk)):
        top = jnp.max(val, axis=0, keepdims=True)
        first = jnp.min(jnp.where(val == top, blk_f, float(n_blk)), axis=0, keepdims=True)
        pick = blk_f == first
        sel = jnp.where(pick, 1.0, sel)
        val = jnp.where(pick, -jnp.inf, val)
    nsel_ref[0, 0] = jnp.where(blk < q0 // SLC_BLOCK, 1.0 - sel, 1.0).astype(BF16)


def _overlap_t(t):
    n_cmp = t // CMP_STRIDE
    n_slc = t // SLC_BLOCK
    cs = np.arange(n_cmp) * CMP_STRIDE
    ss = np.arange(n_slc) * SLC_BLOCK
    ov = np.clip(np.minimum(cs[None, :] + CMP_BLOCK, ss[:, None] + SLC_BLOCK)
                 - np.maximum(cs[None, :], ss[:, None]), 0, None) / CMP_BLOCK
    return jnp.asarray(ov, dtype=BF16)


def _nsa_select(tr, kc, vct):
    bsz, _, t = tr.shape
    g, nc, dh = kc.shape[1:]
    ns = t // SLC_BLOCK
    nq = t // Q_BLOCK
    return pl.pallas_call(
        _nsa_select_kernel,
        out_shape=(jax.ShapeDtypeStruct((bsz, g * NSA_REP, dh, t), F32),
                   jax.ShapeDtypeStruct((bsz, g, ns, t), BF16)),
        grid=(bsz, g, nq),
        in_specs=[pl.BlockSpec((1, GROUP_Q, Q_BLOCK), lambda b, j, i: (b, j, i)),
                  pl.BlockSpec((1, 1, nc, dh), lambda b, j, i: (b, j, 0, 0)),
                  pl.BlockSpec((1, 1, dh, nc), lambda b, j, i: (b, j, 0, 0)),
                  _resident((ns, nc))],
        out_specs=(pl.BlockSpec((1, NSA_REP, dh, Q_BLOCK), lambda b, j, i: (b, j, 0, i)),
                   pl.BlockSpec((1, 1, ns, Q_BLOCK), lambda b, j, i: (b, j, 0, i))),
        compiler_params=_params("parallel", "parallel", "parallel"),
        name="nsa_select",
    )(tr, kc, vct, _overlap_t(t))


def _nsa_main_kernel(qt_ref, nsel_ref, ka_ref, vst_ref, kw_ref, vwt_ref, oct_ref, gt_ref,
                     o_ref, s0_ref, s1_ref, p0_ref, p1_ref, acc_ref, ow_ref, *, key_tile):
    qb = pl.program_id(2)
    q0 = qb * Q_BLOCK
    dh = NSA_HEAD_DIM
    last_tile = ka_ref.shape[2] // key_tile - 1
    qm = _q_lanes(qt_ref)
    nsel = nsel_ref[0, 0]
    q_pad = jnp.concatenate([qm, jnp.zeros((Q_BLOCK - dh, LANES_Q), BF16)], axis=0)
    q_aug = jnp.concatenate([q_pad, jnp.concatenate([nsel] * NSA_REP, axis=1)], axis=0)

    half = LANES_Q // 2
    chunk = key_tile // KEY_CHUNKS
    q_half = [q_aug[:, h * half:(h + 1) * half] for h in range(2)]
    pieces = [(h, c) for c in range(KEY_CHUNKS) for h in range(2)]

    def step(kt_s, s_in, p_out, col_max, m, l, kt_v, p_in, a_v, kt_n, s_out):
        m_new = jnp.maximum(m, col_max)
        a = jnp.exp2(m - m_new)
        kv0 = pl.multiple_of(kt_v * key_tile, key_tile)
        kn0 = pl.multiple_of(kt_n * key_tile, key_tile)
        sums = [None, None]
        new_max = [None, None]
        acc = [acc_ref[:, h * half:(h + 1) * half] * a_v[:, h * half:(h + 1) * half] for h in range(2)]
        for h, c in pieces:
            lanes = slice(h * half, (h + 1) * half)
            rows = slice(c * chunk, (c + 1) * chunk)
            p = jnp.exp2(s_in[rows, lanes] - m_new[:, lanes])
            p_out[rows, lanes] = p.astype(BF16)
            ps = jnp.sum(p, axis=0, keepdims=True)
            sums[h] = ps if sums[h] is None else sums[h] + ps
            acc[h] = acc[h] + jnp.dot(vst_ref[0, :, pl.ds(kv0 + c * chunk, chunk)], p_in[rows, lanes],
                                      preferred_element_type=F32)
            s = jnp.dot(ka_ref[0, 0, pl.ds(kn0 + c * chunk, chunk), :], q_half[h],
                        preferred_element_type=F32)
            s_out[rows, lanes] = s
            cm = jnp.max(s, axis=0, keepdims=True)
            new_max[h] = cm if new_max[h] is None else jnp.maximum(new_max[h], cm)
        for h in range(2):
            acc_ref[:, h * half:(h + 1) * half] = acc[h]
        return (m_new, l * a + jnp.concatenate(sums, axis=1), a, jnp.concatenate(new_max, axis=1))

    def pair(j, carry):
        m, l, a_prev, cm0 = carry
        t0 = 2 * j
        m, l, a0, cm1 = step(t0, s0_ref, p0_ref, cm0, m, l,
                             jnp.maximum(t0 - 1, 0), p1_ref, a_prev, t0 + 1, s1_ref)
        m, l, a1, cm0 = step(t0 + 1, s1_ref, p1_ref, cm1, m, l,
                             t0, p0_ref, a0, jnp.minimum(t0 + 2, last_tile), s0_ref)
        return m, l, a1, cm0

    def scores(kt, s_ref):
        k0 = pl.multiple_of(kt * key_tile, key_tile)
        s = jnp.dot(ka_ref[0, 0, pl.ds(k0, key_tile), :], q_aug, preferred_element_type=F32)
        s_ref[...] = s
        return jnp.max(s, axis=0, keepdims=True)

    def values(kt, p_ref, a):
        k0 = pl.multiple_of(kt * key_tile, key_tile)
        acc_ref[...] = acc_ref[...] * a + jnp.dot(vst_ref[0, :, pl.ds(k0, key_tile)], p_ref[...],
                                                  preferred_element_type=F32)

    p1_ref[...] = jnp.zeros_like(p1_ref)
    cm_first = scores(0, s0_ref)

    d0 = pl.multiple_of(q0, Q_BLOCK)
    s = jnp.dot(ka_ref[0, 0, pl.ds(d0, Q_BLOCK), 0:Q_BLOCK], q_pad, preferred_element_type=F32)
    tq = lax.broadcasted_iota(jnp.int32, s.shape, 1) & (Q_BLOCK - 1)
    kl = lax.broadcasted_iota(jnp.int32, s.shape, 0)
    s = jnp.where(kl <= tq, s, NEG)
    m_own = jnp.max(s, axis=0, keepdims=True)
    p = jnp.exp2(s - m_own)
    acc_ref[...] = jnp.dot(vst_ref[0, :, pl.ds(d0, Q_BLOCK)], p.astype(BF16),
                           preferred_element_type=F32)

    flag_row = jnp.where(lax.broadcasted_iota(jnp.int32, (Q_BLOCK - dh, LANES_Q), 0) == 0, NEG, 0.0)
    q_win = jnp.concatenate([qm, flag_row.astype(BF16)], axis=0)
    sw = jnp.dot(kw_ref[0, 0, pl.ds(d0, WINDOW + Q_BLOCK), :], q_win, preferred_element_type=F32)
    s_old = jnp.where(kl > tq, sw[0:Q_BLOCK], NEG)
    s_mid = sw[Q_BLOCK:WINDOW]
    s_new = jnp.where(kl <= tq, sw[WINDOW:], NEG)
    m_w = jnp.maximum(jnp.max(s_mid, axis=0, keepdims=True),
                      jnp.maximum(jnp.max(s_old, axis=0, keepdims=True),
                                  jnp.max(s_new, axis=0, keepdims=True)))
    e = jnp.exp2(jnp.concatenate([s_old, s_mid, s_new], axis=0) - m_w)
    ow_ref[...] = (jnp.dot(vwt_ref[0, :, pl.ds(d0, WINDOW + Q_BLOCK)], e.astype(BF16),
                           preferred_element_type=F32) / jnp.sum(e, axis=0, keepdims=True))

    n_pairs = ((q0 + key_tile - 1) // key_tile + 1) // 2
    init = (m_own, jnp.sum(p, axis=0, keepdims=True), jnp.ones((1, LANES_Q), F32), cm_first)
    _, l, a_last, _ = lax.fori_loop(0, n_pairs, pair, init)
    values(jnp.maximum(2 * n_pairs - 1, 0), p1_ref, a_last)
    o_s = acc_ref[...] / l
    o_w = ow_ref[...]

    gates = _sigmoid(gt_ref[0])
    outs = []
    for r in range(NSA_REP):
        sl = slice(r * Q_BLOCK, (r + 1) * Q_BLOCK)
        g_c, g_s, g_w = (gates[r * N_BRANCH + i:r * N_BRANCH + i + 1, :] for i in range(N_BRANCH))
        outs.append(g_c * oct_ref[0, r] + g_s * o_s[:, sl] + g_w * o_w[:, sl])
    o_ref[0] = jnp.concatenate(outs, axis=0).T


def _nsa_main(tr, nsel, k_aug, kw, oct_, key_tile):
    bsz, _, t = tr.shape
    g = k_aug.shape[1]
    dh = NSA_HEAD_DIM
    ns = t // SLC_BLOCK
    nq = t // Q_BLOCK
    aw = k_aug.shape[-1]
    assert (t // key_tile) % 2 == 0 and key_tile % Q_BLOCK == 0
    vs_t = tr[:, TR_VS:TR_VW].astype(BF16)
    vw_t = jnp.pad(tr[:, TR_VW:TR_G].astype(BF16), ((0, 0), (0, 0), (WINDOW, 0)))
    tw = t + WINDOW
    return pl.pallas_call(
        functools.partial(_nsa_main_kernel, key_tile=key_tile),
        out_shape=jax.ShapeDtypeStruct((bsz, t, NSA_WIDTH), F32),
        grid=(bsz, g, nq),
        in_specs=[pl.BlockSpec((1, GROUP_Q, Q_BLOCK), lambda b, j, i: (b, j, i)),
                  pl.BlockSpec((1, 1, ns, Q_BLOCK), lambda b, j, i: (b, j, 0, i)),
                  pl.BlockSpec((1, 1, t, aw), lambda b, j, i: (b, j, 0, 0)),
                  pl.BlockSpec((1, dh, t), lambda b, j, i: (b, j, 0)),
                  pl.BlockSpec((1, 1, tw, Q_BLOCK), lambda b, j, i: (b, j, 0, 0)),
                  pl.BlockSpec((1, dh, tw), lambda b, j, i: (b, j, 0)),
                  pl.BlockSpec((1, NSA_REP, dh, Q_BLOCK), lambda b, j, i: (b, j, 0, i)),
                  pl.BlockSpec((1, GATE_ROWS, Q_BLOCK), lambda b, j, i: (b, TR_G // GATE_ROWS + j, i))],
        out_specs=pl.BlockSpec((1, Q_BLOCK, GROUP_Q), lambda b, j, i: (b, i, j)),
        scratch_shapes=[pltpu.VMEM((key_tile, LANES_Q), F32), pltpu.VMEM((key_tile, LANES_Q), F32),
                        pltpu.VMEM((key_tile, LANES_Q), BF16), pltpu.VMEM((key_tile, LANES_Q), BF16),
                        pltpu.VMEM((dh, LANES_Q), F32), pltpu.VMEM((dh, LANES_Q), F32)],
        compiler_params=_params("parallel", "parallel", "arbitrary"),
        name="nsa_main",
    )(tr, nsel, k_aug, vs_t, kw, vw_t, oct_, tr)


def _augment_keys(k_slc):
    bsz, g, t, dh = k_slc.shape
    ns = t // SLC_BLOCK
    own = (np.arange(t)[:, None] // SLC_BLOCK) == np.arange(ns)[None, :]
    bias = jnp.asarray(np.where(own, NEG, 0.0), dtype=BF16)
    return jnp.concatenate([k_slc.astype(BF16), jnp.zeros((bsz, g, t, Q_BLOCK - dh), BF16),
                            jnp.broadcast_to(bias, (bsz, g, t, ns))], axis=-1)


def _window_keys(k_win):
    bsz, g, t, dh = k_win.shape
    flag = jnp.zeros((bsz, g, t, Q_BLOCK - dh), BF16)
    body = jnp.concatenate([k_win.astype(BF16), flag], axis=-1)
    pad_row = jnp.zeros((Q_BLOCK,), BF16).at[dh].set(1.0)
    return jnp.concatenate([jnp.broadcast_to(pad_row, (bsz, g, WINDOW, Q_BLOCK)), body], axis=2)


def _retention_kernel(q_ref, k_ref, v_ref, gate_ref, cos_ref, sin_ref, gng_ref, gnb_ref,
                      o_ref, state_ref):
    c = RET_CHUNK
    d = RET_HEAD_DIM

    @pl.when(pl.program_id(1) == 0)
    def _():
        state_ref[...] = jnp.zeros_like(state_ref)

    cos = cos_ref[...]
    sin = sin_ref[...]
    i_row = lax.broadcasted_iota(jnp.int32, (c, c), 0)
    i_col = lax.broadcasted_iota(jnp.int32, (c, c), 1)
    rel = (i_row - i_col).astype(F32)
    pos = lax.broadcasted_iota(jnp.int32, (c, 1), 0).astype(F32)
    for h in range(RET_HEADS):
        log_g = math.log(1.0 - 2.0 ** (-5.0 - h))
        sl = slice(h * d, (h + 1) * d)
        q = q_ref[0, :, sl]
        k = k_ref[0, :, sl]
        q = q * cos + pltpu.roll(q, d // 2, 1) * sin
        k = (k * cos + pltpu.roll(k, d // 2, 1) * sin) * (d ** -0.5)
        vb = v_ref[0, :, sl].astype(BF16)
        decay = jnp.where(rel >= 0, jnp.exp(jnp.maximum(rel, 0.0) * log_g), 0.0)
        inner = lax.dot_general(q.astype(BF16), k.astype(BF16), (((1,), (1,)), ((), ())),
                                preferred_element_type=F32) * decay
        o = jnp.dot(inner.astype(BF16), vb, preferred_element_type=F32)
        state = state_ref[h]
        xi = jnp.exp((pos + 1.0) * log_g)
        o = o + jnp.dot((q * xi).astype(BF16), state.astype(BF16), preferred_element_type=F32)
        zeta = jnp.exp((c - 1.0 - pos) * log_g)
        kv = jnp.dot((k * zeta).T.astype(BF16), vb, preferred_element_type=F32)
        state_ref[h] = math.exp(c * log_g) * state + kv
        mu = jnp.mean(o, axis=-1, keepdims=True)
        oc = o - mu
        var = jnp.mean(oc * oc, axis=-1, keepdims=True)
        y = oc * lax.rsqrt(var + LN_EPS) * gng_ref[:, sl] + gnb_ref[:, sl]
        gate = gate_ref[0, :, sl]
        o_ref[0, :, sl] = gate * _sigmoid(gate) * y


def _rotary_tables(t):
    d = RET_HEAD_DIM
    freqs = ROPE_BASE ** (-jnp.arange(0, d, 2, dtype=F32) / d)
    ang = jnp.arange(t, dtype=F32)[:, None] * freqs[None, :]
    cos, sin = jnp.cos(ang), jnp.sin(ang)
    return jnp.concatenate([cos, cos], axis=1), jnp.concatenate([-sin, sin], axis=1)


def _retention(nat, gn_g, gn_b):
    bsz, t, _ = nat.shape
    c = RET_CHUNK
    cos2, sin2 = _rotary_tables(t)
    col0 = NAT_RET // RET_WIDTH
    part = lambda j: pl.BlockSpec((1, c, RET_WIDTH), lambda b, n: (b, n, col0 + j))
    tab = pl.BlockSpec((c, RET_HEAD_DIM), lambda b, n: (n, 0))
    return pl.pallas_call(
        _retention_kernel,
        out_shape=jax.ShapeDtypeStruct((bsz, t, RET_WIDTH), F32),
        grid=(bsz, t // c),
        in_specs=[part(0), part(1), part(2), part(3), tab, tab,
                  _resident((1, RET_WIDTH)), _resident((1, RET_WIDTH))],
        out_specs=pl.BlockSpec((1, c, RET_WIDTH), lambda b, n: (b, n, 0)),
        scratch_shapes=[pltpu.VMEM((RET_HEADS, RET_HEAD_DIM, RET_HEAD_DIM), F32)],
        compiler_params=_params("parallel", "arbitrary"),
        name="retention",
    )(nat, nat, nat, nat, cos2, sin2, gn_g.reshape(1, -1), gn_b.reshape(1, -1))


def _out_ln_kernel(x_ref, on_ref, or_ref, wn_ref, wr_ref, g_ref, b_ref, o_ref):
    mix = (jnp.dot(on_ref[...].astype(BF16), wn_ref[...], preferred_element_type=F32)
           + jnp.dot(or_ref[...].astype(BF16), wr_ref[...], preferred_element_type=F32))
    o_ref[...] = _layer_norm(ALPHA * x_ref[...] + mix, g_ref[...], b_ref[...])


def _out_ln(x2d, o_nsa, o_ret, w_out, g, b):
    n = x2d.shape[0]
    row = lambda w: pl.BlockSpec((ROW_TILE, w), lambda i: (i, 0))
    return pl.pallas_call(
        _out_ln_kernel,
        out_shape=jax.ShapeDtypeStruct((n, D_MODEL), F32),
        grid=(n // ROW_TILE,),
        in_specs=[row(D_MODEL), row(NSA_WIDTH), row(RET_WIDTH),
                  _resident((NSA_WIDTH, D_MODEL)), _resident((RET_WIDTH, D_MODEL)),
                  _resident((1, D_MODEL)), _resident((1, D_MODEL))],
        out_specs=row(D_MODEL),
        compiler_params=_params("parallel"),
        name="out_ln",
    )(x2d, o_nsa, o_ret, w_out[:NSA_WIDTH].astype(BF16), w_out[NSA_WIDTH:].astype(BF16),
      g.reshape(1, -1), b.reshape(1, -1))


def _ple_ln_kernel(x_ref, p_ref, wp_ref, wg_ref, g_ref, b_ref, o_ref):
    x = x_ref[...]
    e = (jnp.dot(p_ref[...].astype(BF16), wp_ref[...], preferred_element_type=F32)
         * _sigmoid(jnp.dot(x.astype(BF16), wg_ref[...], preferred_element_type=F32)))
    o_ref[...] = _layer_norm(ALPHA * x + e, g_ref[...], b_ref[...])


def _ple_ln(x2d, p2d, w_ple, w_gate, g, b):
    n = x2d.shape[0]
    row = lambda w: pl.BlockSpec((ROW_TILE, w), lambda i: (i, 0))
    return pl.pallas_call(
        _ple_ln_kernel,
        out_shape=jax.ShapeDtypeStruct((n, D_MODEL), F32),
        grid=(n // ROW_TILE,),
        in_specs=[row(D_MODEL), row(PLE_DIM), _resident((PLE_DIM, D_MODEL)),
                  _resident((D_MODEL, D_MODEL)), _resident((1, D_MODEL)), _resident((1, D_MODEL))],
        out_specs=row(D_MODEL),
        compiler_params=_params("parallel"),
        name="ple_ln",
    )(x2d, p2d, w_ple.astype(BF16), w_gate.astype(BF16), g.reshape(1, -1), b.reshape(1, -1))


def _token_mixer(x, w_in, cmp_pos, cmp_w1, cmp_w2, gn_g, gn_b):
    bsz, t, _ = x.shape
    g, dh = NSA_GROUPS, NSA_HEAD_DIM
    nat, tr = _in_proj(x, w_in)
    by_group = lambda c0: nat[:, :, c0:c0 + g * dh].reshape(bsz, t, g, dh).transpose(0, 2, 1, 3)
    half_blocks = lambda a: a.reshape(bsz, g, t // CMP_STRIDE, CMP_STRIDE * dh)
    kc, vct = _compress(half_blocks(by_group(NAT_KC)), half_blocks(by_group(NAT_VC)),
                        cmp_pos, cmp_w1, cmp_w2)
    oct_, nsel = _nsa_select(tr, kc, vct)
    key_tile = min(512, t)
    o_nsa = _nsa_main(tr, nsel, _augment_keys(by_group(NAT_KS)), _window_keys(by_group(NAT_KW)),
                      oct_, key_tile)
    o_ret = _retention(nat, gn_g, gn_b)
    return o_nsa, o_ret


def kernel(x, p, ffn_w13, ffn_w2, w_in, cmp_pos, cmp_w1, cmp_w2, ret_gn_g, ret_gn_b,
           w_out, w_ple, w_ple_gate, ln_g, ln_b):
    bsz, t, d = x.shape
    n = bsz * t
    for i in range(ffn_w13.shape[0]):
        h = _ffn_ln(x.reshape(n, d), ffn_w13[i, 0], ffn_w2[i, 0], ln_g[i, 0], ln_b[i, 0])
        o_nsa, o_ret = _token_mixer(h.reshape(bsz, t, d), w_in[i], cmp_pos[i], cmp_w1[i], cmp_w2[i],
                                    ret_gn_g[i], ret_gn_b[i])
        h = _out_ln(h, o_nsa.reshape(n, -1), o_ret.reshape(n, -1), w_out[i], ln_g[i, 1], ln_b[i, 1])
        h = _ffn_ln(h, ffn_w13[i, 1], ffn_w2[i, 1], ln_g[i, 2], ln_b[i, 2])
        h = _ple_ln(h, p[i].reshape(n, -1), w_ple[i], w_ple_gate[i], ln_g[i, 3], ln_b[i, 3])
        x = h.reshape(bsz, t, d)
    return x
```

```python
import functools
import math

import jax
import jax.numpy as jnp
import numpy as np
from jax import lax
from jax.experimental import pallas as pl
from jax.experimental.pallas import tpu as pltpu

F32 = jnp.float32
BF16 = jnp.bfloat16

D_MODEL = 1024
D_FF = 2816
NSA_GROUPS = 2
NSA_REP = 4
NSA_HEAD_DIM = 64
N_BRANCH = 3
CMP_BLOCK = 32
CMP_STRIDE = 16
CMP_HIDDEN = 256
SLC_BLOCK = 64
SLC_TOPK = 16
WINDOW = 512
Q_BLOCK = 128
RET_HEADS = 4
RET_HEAD_DIM = 128
RET_CHUNK = 128
ROPE_BASE = 10000.0
PLE_DIM = 256
DEPTH = 1
ALPHA = (2.0 * DEPTH) ** 0.25
LN_EPS = 1e-5
NEG = -1e30
LOG2_E = math.log2(math.e)

NSA_WIDTH = NSA_GROUPS * NSA_REP * NSA_HEAD_DIM
RET_WIDTH = RET_HEADS * RET_HEAD_DIM
GROUP_Q = NSA_REP * NSA_HEAD_DIM
LANES_Q = NSA_REP * Q_BLOCK
GATE_ROWS = 16
KV_WIDTH = NSA_GROUPS * NSA_HEAD_DIM
TR_Q, TR_VS, TR_VW, TR_G = 0, 512, 640, 768
TR_ROWS = TR_G + NSA_GROUPS * GATE_ROWS
NAT_CMP, NAT_RET = 0, 2 * KV_WIDTH
NAT_KS = NAT_RET + 4 * RET_WIDTH
NAT_KW = NAT_KS + NSA_GROUPS * Q_BLOCK
NAT_COLS = NAT_KW + NSA_GROUPS * Q_BLOCK

KEY_CHUNKS = 1
SUM_ROWS = 16
SEL_BLOCKS = 4
VMEM_LIMIT = 56 * 1024 * 1024
ROW_TILE = 512


def _layer_norm(z, g, b):
    mu = jnp.mean(z, axis=-1, keepdims=True)
    zc = z - mu
    var = jnp.mean(zc * zc, axis=-1, keepdims=True)
    return zc * lax.rsqrt(var + LN_EPS) * g + b


def _sigmoid(v):
    return 1.0 / (1.0 + jnp.exp(-v))


def _params(*sem):
    return pltpu.CompilerParams(dimension_semantics=sem, vmem_limit_bytes=VMEM_LIMIT)


def _resident(shape):
    nd = len(shape)
    return pl.BlockSpec(shape, lambda *_: (0,) * nd, pipeline_mode=pl.Buffered(1))


def _ffn_ln_kernel(x_ref, w13_ref, w2_ref, g_ref, b_ref, o_ref):
    x = x_ref[...]
    xb = x.astype(BF16)
    a = jnp.dot(xb, w13_ref[:, :D_FF], preferred_element_type=F32)
    u = jnp.dot(xb, w13_ref[:, D_FF:], preferred_element_type=F32)
    h = (a * _sigmoid(a) * u).astype(BF16)
    y = jnp.dot(h, w2_ref[...], preferred_element_type=F32)
    o_ref[...] = _layer_norm(ALPHA * x + 0.5 * y, g_ref[...], b_ref[...])


def _ffn_ln(x2d, w13, w2, g, b):
    n = x2d.shape[0]
    row = pl.BlockSpec((ROW_TILE, D_MODEL), lambda i: (i, 0))
    return pl.pallas_call(
        _ffn_ln_kernel,
        out_shape=jax.ShapeDtypeStruct((n, D_MODEL), F32),
        grid=(n // ROW_TILE,),
        in_specs=[row, _resident((D_MODEL, 2 * D_FF)),
                  _resident((D_FF, D_MODEL)), _resident((1, D_MODEL)), _resident((1, D_MODEL))],
        out_specs=row,
        compiler_params=_params("parallel"),
        name="ffn_ln",
    )(x2d, w13, w2, g.reshape(1, -1), b.reshape(1, -1))


def _in_proj_kernel(x_ref, wn_ref, wt_ref, cmp_ref, ret_ref, ka_ref, kw_ref, qt_ref, vst_ref, vwt_ref, gt_ref):
    xb = x_ref[0].astype(BF16)
    tm = xb.shape[0]
    nat = jnp.dot(xb, wn_ref[...], preferred_element_type=F32)
    cmp_ref[0] = nat[:, NAT_CMP:NAT_RET]
    ret_ref[0] = nat[:, NAT_RET:NAT_KS]
    ns = ka_ref.shape[3] - Q_BLOCK
    row = pl.program_id(1) * tm + lax.broadcasted_iota(jnp.int32, (tm, ns), 0)
    own = row // SLC_BLOCK == lax.broadcasted_iota(jnp.int32, (tm, ns), 1)
    bias = jnp.where(own, NEG, 0.0).astype(BF16)
    for g in range(NSA_GROUPS):
        ka_ref[0, g, :, 0:Q_BLOCK] = nat[:, NAT_KS + g * Q_BLOCK:NAT_KS + (g + 1) * Q_BLOCK].astype(BF16)
        ka_ref[0, g, :, Q_BLOCK:] = bias
        kw_ref[0, g] = nat[:, NAT_KW + g * Q_BLOCK:NAT_KW + (g + 1) * Q_BLOCK].astype(BF16)
    tr = lax.dot_general(wt_ref[...], xb, (((1,), (1,)), ((), ())), preferred_element_type=F32)
    qt_ref[0] = (tr[TR_Q:TR_VS] * (NSA_HEAD_DIM ** -0.5 * LOG2_E)).astype(BF16)
    vst_ref[0] = tr[TR_VS:TR_VW].astype(BF16)
    vwt_ref[0] = tr[TR_VW:TR_G].astype(BF16)
    gt_ref[0] = tr[TR_G:]


def _split_w_in(w_in):
    c_q = NSA_WIDTH
    c_kv = c_q + 2 * N_BRANCH * KV_WIDTH
    c_g = c_kv + NSA_GROUPS * NSA_REP * N_BRANCH
    kv = w_in[:, c_q:c_kv].reshape(D_MODEL, 2 * N_BRANCH, KV_WIDTH)
    gates = w_in[:, c_kv:c_g].reshape(D_MODEL, NSA_GROUPS, NSA_REP * N_BRANCH)
    gates = jnp.pad(gates, ((0, 0), (0, 0), (0, GATE_ROWS - NSA_REP * N_BRANCH)))
    pad_groups = lambda w: jnp.pad(w.reshape(D_MODEL, NSA_GROUPS, NSA_HEAD_DIM),
                                   ((0, 0), (0, 0), (0, Q_BLOCK - NSA_HEAD_DIM))).reshape(D_MODEL, -1)
    w_nat = jnp.concatenate([kv[:, 0], kv[:, 1], w_in[:, c_g:], pad_groups(kv[:, 2]), pad_groups(kv[:, 4])],
                            axis=1)
    w_tr = jnp.concatenate([w_in[:, :c_q], kv[:, 3], kv[:, 5],
                            gates.reshape(D_MODEL, NSA_GROUPS * GATE_ROWS)], axis=1)
    return w_nat.astype(BF16), w_tr.T.astype(BF16)


def _in_proj(x, w_in):
    bsz, t, _ = x.shape
    g = NSA_GROUPS
    ns = t // SLC_BLOCK
    w_nat, w_trt = _split_w_in(w_in)
    rows = lambda w: pl.BlockSpec((1, ROW_TILE, w), lambda b, i: (b, i, 0))
    cols = lambda h: pl.BlockSpec((1, h, ROW_TILE), lambda b, i: (b, 0, i))
    grouped = lambda w: pl.BlockSpec((1, g, ROW_TILE, w), lambda b, i: (b, 0, i, 0))
    return pl.pallas_call(
        _in_proj_kernel,
        out_shape=(jax.ShapeDtypeStruct((bsz, t, NAT_RET - NAT_CMP), F32),
                   jax.ShapeDtypeStruct((bsz, t, NAT_KS - NAT_RET), F32),
                   jax.ShapeDtypeStruct((bsz, g, t, Q_BLOCK + ns), BF16),
                   jax.ShapeDtypeStruct((bsz, g, t, Q_BLOCK), BF16),
                   jax.ShapeDtypeStruct((bsz, TR_VS - TR_Q, t), BF16),
                   jax.ShapeDtypeStruct((bsz, TR_VW - TR_VS, t), BF16),
                   jax.ShapeDtypeStruct((bsz, TR_G - TR_VW, t), BF16),
                   jax.ShapeDtypeStruct((bsz, TR_ROWS - TR_G, t), F32)),
        grid=(bsz, t // ROW_TILE),
        in_specs=[rows(D_MODEL), _resident((D_MODEL, NAT_COLS)), _resident((TR_ROWS, D_MODEL))],
        out_specs=(rows(NAT_RET - NAT_CMP), rows(NAT_KS - NAT_RET), grouped(Q_BLOCK + ns), grouped(Q_BLOCK),
                   cols(TR_VS - TR_Q), cols(TR_VW - TR_VS), cols(TR_G - TR_VW), cols(TR_ROWS - TR_G)),
        compiler_params=_params("parallel", "parallel"),
        name="in_proj",
    )(x, w_nat, w_trt)


def _gelu_tanh(v):
    return 0.5 * v * (1.0 + jnp.tanh(math.sqrt(2.0 / math.pi) * (v + 0.044715 * (v * v * v))))


def _compress_hidden(h, pos_ref, w1_ref):
    top = jnp.dot((h + pos_ref[0:1, :]).astype(BF16), w1_ref[0], preferred_element_type=F32)
    bot = jnp.dot((h + pos_ref[1:2, :]).astype(BF16), w1_ref[1], preferred_element_type=F32)
    nc = h.shape[0]
    return _gelu_tanh(top + pltpu.roll(bot, nc - 1, 0)).astype(BF16)


def _compress_kernel(hk_ref, hv_ref, pk_ref, pv_ref, w1k_ref, w1v_ref, w2k_ref, w2vt_ref,
                     kc_ref, vct_ref):
    nc = hk_ref.shape[2]
    gk = _compress_hidden(hk_ref[0, 0], pk_ref, w1k_ref)
    kc = jnp.dot(gk, w2k_ref[...], preferred_element_type=F32)
    row = lax.broadcasted_iota(jnp.int32, kc.shape, 0)
    kc_ref[0, 0] = jnp.where(row < nc - 1, kc, 0.0).astype(BF16)
    gv = _compress_hidden(hv_ref[0, 0], pv_ref, w1v_ref)
    vct = lax.dot_general(w2vt_ref[...], gv, (((1,), (1,)), ((), ())),
                          preferred_element_type=F32)
    col = lax.broadcasted_iota(jnp.int32, vct.shape, 1)
    vct_ref[0, 0, 0:NSA_HEAD_DIM, :] = jnp.where(col < nc - 1, vct, 0.0).astype(BF16)
    vct_ref[0, 0, NSA_HEAD_DIM:, :] = jnp.ones((SUM_ROWS, nc), BF16)


def _compress(hk, hv, cmp_pos, cmp_w1, cmp_w2):
    bsz, g, nc, hw = hk.shape
    dh = NSA_HEAD_DIM
    half = CMP_BLOCK // 2
    pos = cmp_pos.reshape(2, 2, half * dh)
    w1 = cmp_w1.reshape(2, 2, half * dh, CMP_HIDDEN).astype(BF16)
    blk = pl.BlockSpec((1, 1, nc, hw), lambda b, j: (b, j, 0, 0))
    return pl.pallas_call(
        _compress_kernel,
        out_shape=(jax.ShapeDtypeStruct((bsz, g, nc, dh), BF16),
                   jax.ShapeDtypeStruct((bsz, g, dh + SUM_ROWS, nc), BF16)),
        grid=(bsz, g),
        in_specs=[blk, blk, _resident((2, hw)), _resident((2, hw)),
                  _resident((2, hw, CMP_HIDDEN)), _resident((2, hw, CMP_HIDDEN)),
                  _resident((CMP_HIDDEN, dh)), _resident((dh, CMP_HIDDEN))],
        out_specs=(pl.BlockSpec((1, 1, nc, dh), lambda b, j: (b, j, 0, 0)),
                   pl.BlockSpec((1, 1, dh + SUM_ROWS, nc), lambda b, j: (b, j, 0, 0))),
        compiler_params=_params("parallel", "parallel"),
        name="compress",
    )(hk, hv, pos[0], pos[1], w1[0], w1[1], cmp_w2[0].astype(BF16), cmp_w2[1].T.astype(BF16))


def _q_lanes(qt_ref, j=0):
    qt = qt_ref[0, :, j * Q_BLOCK:(j + 1) * Q_BLOCK]
    return jnp.concatenate([qt[r * NSA_HEAD_DIM:(r + 1) * NSA_HEAD_DIM, :] for r in range(NSA_REP)], axis=1)


def _lane_token(shape, q0):
    return q0 + (lax.broadcasted_iota(jnp.int32, shape, 1) & (Q_BLOCK - 1))


def _nsa_select_kernel(qt_ref, kc_ref, vct_ref, ovl_ref, bias_ref, oct_ref, nsel_ref, imp_ref):
    step = pl.program_id(2)
    dh = NSA_HEAD_DIM
    nc = kc_ref.shape[2]
    kc = kc_ref[0, 0]
    lhs = jnp.concatenate([ovl_ref[...], vct_ref[0, 0]], axis=0)
    n_blk = ovl_ref.shape[0]

    def scores(j):
        qb = step * SEL_BLOCKS + j
        s = jnp.dot(kc, _q_lanes(qt_ref, j), preferred_element_type=F32)
        bias = bias_ref[pl.ds(pl.multiple_of(nc - qb * (Q_BLOCK // CMP_STRIDE), 8), nc), :]
        return s + jnp.concatenate([bias] * NSA_REP, axis=1)

    def select(pair):
        lanes = slice(pair * 2 * Q_BLOCK, (pair + 1) * 2 * Q_BLOCK)
        imp = imp_ref[:, lanes]
        blk = lax.broadcasted_iota(jnp.int32, imp.shape, 0)
        t = ((step * SEL_BLOCKS + 2 * pair) * Q_BLOCK
             + lax.broadcasted_iota(jnp.int32, imp.shape, 1))
        cur = t // SLC_BLOCK
        valid = blk <= cur
        forced = (blk == 0) | (blk >= cur - 1)
        val0 = jnp.where(valid & jnp.logical_not(forced), imp, -jnp.inf)
        blk_f = blk.astype(F32)
        val = val0
        for _ in range(max(min(SLC_TOPK, n_blk) - 3, 0)):
            top = jnp.max(val, axis=0, keepdims=True)
            first = jnp.min(jnp.where(val == top, blk_f, float(n_blk)), axis=0, keepdims=True)
            val = jnp.where(blk_f == first, -jnp.inf, val)
            yield
        chosen = valid & (forced | (val != val0))
        nsel_ref[0, 0, :, lanes] = jnp.where(chosen & (blk < 2 * (t // Q_BLOCK)), 0.0, 1.0).astype(BF16)

    running = []

    def advance(n):
        for _ in range(n):
            if running and next(running[0], True):
                running.pop(0)

    s_next = scores(0)
    for j in range(SEL_BLOCKS):
        qb = step * SEL_BLOCKS + j
        s = s_next
        if j + 1 < SEL_BLOCKS:
            s_next = scores(j + 1)
        advance(3)
        e = jnp.exp2(s - jnp.max(s, axis=0, keepdims=True)).astype(BF16)
        advance(3)
        r = jnp.dot(lhs, e, preferred_element_type=F32)
        tok = _lane_token((1, LANES_Q), qb * Q_BLOCK)
        scale = jnp.where(tok >= CMP_BLOCK - 1, 1.0 / r[n_blk + dh:n_blk + dh + 1], 0.0)
        o = r[n_blk:n_blk + dh] * scale
        w = r[0:n_blk] * scale
        imp = w[:, 0:Q_BLOCK]
        for h in range(NSA_REP):
            oct_ref[0, h, :, j * Q_BLOCK:(j + 1) * Q_BLOCK] = o[:, h * Q_BLOCK:(h + 1) * Q_BLOCK]
            if h:
                imp = imp + w[:, h * Q_BLOCK:(h + 1) * Q_BLOCK]
        imp_ref[:, j * Q_BLOCK:(j + 1) * Q_BLOCK] = imp
        advance(3)
        if j % 2 == 1:
            running.append(select(j // 2))
    while running:
        advance(1)


def _cmp_bias_table(t):
    nc = t // CMP_STRIDE
    rel = (np.arange(2 * nc)[:, None] - nc) * CMP_STRIDE + CMP_BLOCK - 1
    return jnp.asarray(np.where(rel <= np.arange(Q_BLOCK)[None, :], 0.0, NEG), dtype=F32)


def _overlap_t(t):
    n_cmp = t // CMP_STRIDE
    n_slc = t // SLC_BLOCK
    cs = np.arange(n_cmp) * CMP_STRIDE
    ss = np.arange(n_slc) * SLC_BLOCK
    ov = np.clip(np.minimum(cs[None, :] + CMP_BLOCK, ss[:, None] + SLC_BLOCK)
                 - np.maximum(cs[None, :], ss[:, None]), 0, None) / CMP_BLOCK
    return jnp.asarray(ov, dtype=BF16)


def _nsa_select(qt, kc, vct):
    bsz, _, t = qt.shape
    g, nc, dh = kc.shape[1:]
    ns = t // SLC_BLOCK
    sq = SEL_BLOCKS * Q_BLOCK
    assert ns % 16 == 0 and t % sq == 0 and SEL_BLOCKS % 2 == 0
    return pl.pallas_call(
        _nsa_select_kernel,
        out_shape=(jax.ShapeDtypeStruct((bsz, g * NSA_REP, dh, t), F32),
                   jax.ShapeDtypeStruct((bsz, g, ns, t), BF16)),
        grid=(bsz, g, t // sq),
        in_specs=[pl.BlockSpec((1, GROUP_Q, sq), lambda b, j, i: (b, j, i)),
                  pl.BlockSpec((1, 1, nc, dh), lambda b, j, i: (b, j, 0, 0)),
                  pl.BlockSpec((1, 1, dh + SUM_ROWS, nc), lambda b, j, i: (b, j, 0, 0)),
                  _resident((ns, nc)), _resident((2 * nc, Q_BLOCK))],
        out_specs=(pl.BlockSpec((1, NSA_REP, dh, sq), lambda b, j, i: (b, j, 0, i)),
                   pl.BlockSpec((1, 1, ns, sq), lambda b, j, i: (b, j, 0, i))),
        scratch_shapes=[pltpu.VMEM((ns, sq), F32)],
        compiler_params=_params("parallel", "parallel", "parallel"),
        name="nsa_select",
    )(qt, kc, vct, _overlap_t(t), _cmp_bias_table(t))


def _nsa_main_kernel(qt_ref, nsel_ref, ka_ref, vst_ref, kw_ref, vwt_ref, oct_ref, gt_ref,
                     o_ref, s0_ref, s1_ref, p0_ref, p1_ref, acc_ref, ow_ref, *, key_tile):
    qb = pl.program_id(2)
    q0 = qb * Q_BLOCK
    dh = NSA_HEAD_DIM
    last_tile = ka_ref.shape[2] // key_tile - 1
    qm = _q_lanes(qt_ref)
    nsel = nsel_ref[0, 0]
    q_pad = jnp.concatenate([qm, jnp.zeros((Q_BLOCK - dh, LANES_Q), BF16)], axis=0)
    q_aug = jnp.concatenate([q_pad, jnp.concatenate([nsel] * NSA_REP, axis=1)], axis=0)

    half = LANES_Q // 2
    chunk = key_tile // KEY_CHUNKS
    q_half = [q_aug[:, h * half:(h + 1) * half] for h in range(2)]
    pieces = [(h, c) for c in range(KEY_CHUNKS) for h in range(2)]

    def step(s_in, p_out, col_max, m, l, kt_v, p_in, a_v, kt_n, s_out):
        m_new = jnp.maximum(m, col_max)
        a = jnp.exp2(m - m_new)
        kv0 = pl.multiple_of(kt_v * key_tile, key_tile)
        kn0 = pl.multiple_of(kt_n * key_tile, key_tile)
        sums = [None, None]
        new_max = [None, None]
        acc = [acc_ref[:, h * half:(h + 1) * half] * a_v[:, h * half:(h + 1) * half] for h in range(2)]
        for h, c in pieces:
            lanes = slice(h * half, (h + 1) * half)
            rows = slice(c * chunk, (c + 1) * chunk)
            p = jnp.exp2(s_in[rows, lanes] - m_new[:, lanes])
            p_out[rows, lanes] = p.astype(BF16)
            ps = jnp.sum(p, axis=0, keepdims=True)
            sums[h] = ps if sums[h] is None else sums[h] + ps
            acc[h] = acc[h] + jnp.dot(vst_ref[0, :, pl.ds(kv0 + c * chunk, chunk)], p_in[rows, lanes],
                                      preferred_element_type=F32)
            s = jnp.dot(ka_ref[0, 0, pl.ds(kn0 + c * chunk, chunk), :], q_half[h],
                        preferred_element_type=F32)
            s_out[rows, lanes] = s
            cm = jnp.max(s, axis=0, keepdims=True)
            new_max[h] = cm if new_max[h] is None else jnp.maximum(new_max[h], cm)
        for h in range(2):
            acc_ref[:, h * half:(h + 1) * half] = acc[h]
        return (m_new, l * a + jnp.concatenate(sums, axis=1), a, jnp.concatenate(new_max, axis=1))

    def pair(j, carry):
        m, l, a_prev, cm0 = carry
        t0 = 2 * j
        m, l, a0, cm1 = step(s0_ref, p0_ref, cm0, m, l,
                             jnp.maximum(t0 - 1, 0), p1_ref, a_prev, t0 + 1, s1_ref)
        m, l, a1, cm0 = step(s1_ref, p1_ref, cm1, m, l,
                             t0, p0_ref, a0, jnp.minimum(t0 + 2, last_tile), s0_ref)
        return m, l, a1, cm0

    def scores(kt, s_ref):
        k0 = pl.multiple_of(kt * key_tile, key_tile)
        s = jnp.dot(ka_ref[0, 0, pl.ds(k0, key_tile), :], q_aug, preferred_element_type=F32)
        s_ref[...] = s
        return jnp.max(s, axis=0, keepdims=True)

    def values(kt, p_ref, a):
        k0 = pl.multiple_of(kt * key_tile, key_tile)
        acc_ref[...] = acc_ref[...] * a + jnp.dot(vst_ref[0, :, pl.ds(k0, key_tile)], p_ref[...],
                                                  preferred_element_type=F32)

    p1_ref[...] = jnp.zeros_like(p1_ref)
    cm_first = scores(0, s0_ref)

    d0 = pl.multiple_of(q0, Q_BLOCK)
    s = jnp.dot(ka_ref[0, 0, pl.ds(d0, Q_BLOCK), 0:Q_BLOCK], q_pad, preferred_element_type=F32)
    tq = lax.broadcasted_iota(jnp.int32, s.shape, 1) & (Q_BLOCK - 1)
    kl = lax.broadcasted_iota(jnp.int32, s.shape, 0)
    s = jnp.where(kl <= tq, s, NEG)
    m_own = jnp.max(s, axis=0, keepdims=True)
    p = jnp.exp2(s - m_own)
    acc_ref[...] = jnp.dot(vst_ref[0, :, pl.ds(d0, Q_BLOCK)], p.astype(BF16),
                           preferred_element_type=F32)

    flag_row = jnp.where(lax.broadcasted_iota(jnp.int32, (Q_BLOCK - dh, LANES_Q), 0) == 0, NEG, 0.0)
    q_win = jnp.concatenate([qm, flag_row.astype(BF16)], axis=0)
    sw = jnp.dot(kw_ref[0, 0, pl.ds(d0, WINDOW + Q_BLOCK), :], q_win, preferred_element_type=F32)
    s_old = jnp.where(kl > tq, sw[0:Q_BLOCK], NEG)
    s_mid = sw[Q_BLOCK:WINDOW]
    s_new = jnp.where(kl <= tq, sw[WINDOW:], NEG)
    m_w = jnp.maximum(jnp.max(s_mid, axis=0, keepdims=True),
                      jnp.maximum(jnp.max(s_old, axis=0, keepdims=True),
                                  jnp.max(s_new, axis=0, keepdims=True)))
    e = jnp.exp2(jnp.concatenate([s_old, s_mid, s_new], axis=0) - m_w)
    ow_ref[...] = (jnp.dot(vwt_ref[0, :, pl.ds(d0, WINDOW + Q_BLOCK)], e.astype(BF16),
                           preferred_element_type=F32) / jnp.sum(e, axis=0, keepdims=True))

    n_pairs = ((q0 + key_tile - 1) // key_tile + 1) // 2
    init = (m_own, jnp.sum(p, axis=0, keepdims=True), jnp.ones((1, LANES_Q), F32), cm_first)
    _, l, a_last, _ = lax.fori_loop(0, n_pairs, pair, init)
    values(jnp.maximum(2 * n_pairs - 1, 0), p1_ref, a_last)
    o_s = acc_ref[...] / l
    o_w = ow_ref[...]

    gates = _sigmoid(gt_ref[0])
    outs = []
    for r in range(NSA_REP):
        sl = slice(r * Q_BLOCK, (r + 1) * Q_BLOCK)
        g_c, g_s, g_w = (gates[r * N_BRANCH + i:r * N_BRANCH + i + 1, :] for i in range(N_BRANCH))
        outs.append(g_c * oct_ref[0, r] + g_s * o_s[:, sl] + g_w * o_w[:, sl])
    o_ref[0] = jnp.concatenate(outs, axis=0).T


def _nsa_main(qt, nsel, k_aug, kw, vs_t, vw_t, oct_, gt, key_tile):
    bsz, _, t = qt.shape
    g = k_aug.shape[1]
    dh = NSA_HEAD_DIM
    ns = t // SLC_BLOCK
    nq = t // Q_BLOCK
    aw = k_aug.shape[-1]
    assert (t // key_tile) % 2 == 0 and key_tile % Q_BLOCK == 0
    pad_row = jnp.zeros((Q_BLOCK,), BF16).at[dh].set(1.0)
    kw = jnp.concatenate([jnp.broadcast_to(pad_row, (bsz, g, WINDOW, Q_BLOCK)), kw], axis=2)
    vw_t = jnp.pad(vw_t, ((0, 0), (0, 0), (WINDOW, 0)))
    tw = t + WINDOW
    return pl.pallas_call(
        functools.partial(_nsa_main_kernel, key_tile=key_tile),
        out_shape=jax.ShapeDtypeStruct((bsz, t, NSA_WIDTH), F32),
        grid=(bsz, g, nq),
        in_specs=[pl.BlockSpec((1, GROUP_Q, Q_BLOCK), lambda b, j, i: (b, j, i)),
                  pl.BlockSpec((1, 1, ns, Q_BLOCK), lambda b, j, i: (b, j, 0, i)),
                  pl.BlockSpec((1, 1, t, aw), lambda b, j, i: (b, j, 0, 0)),
                  pl.BlockSpec((1, dh, t), lambda b, j, i: (b, j, 0)),
                  pl.BlockSpec((1, 1, tw, Q_BLOCK), lambda b, j, i: (b, j, 0, 0)),
                  pl.BlockSpec((1, dh, tw), lambda b, j, i: (b, j, 0)),
                  pl.BlockSpec((1, NSA_REP, dh, Q_BLOCK), lambda b, j, i: (b, j, 0, i)),
                  pl.BlockSpec((1, GATE_ROWS, Q_BLOCK), lambda b, j, i: (b, j, i))],
        out_specs=pl.BlockSpec((1, Q_BLOCK, GROUP_Q), lambda b, j, i: (b, i, j)),
        scratch_shapes=[pltpu.VMEM((key_tile, LANES_Q), F32), pltpu.VMEM((key_tile, LANES_Q), F32),
                        pltpu.VMEM((key_tile, LANES_Q), BF16), pltpu.VMEM((key_tile, LANES_Q), BF16),
                        pltpu.VMEM((dh, LANES_Q), F32), pltpu.VMEM((dh, LANES_Q), F32)],
        compiler_params=_params("parallel", "parallel", "arbitrary"),
        name="nsa_main",
    )(qt, nsel, k_aug, vs_t, kw, vw_t, oct_, gt)


def _retention_kernel(q_ref, k_ref, v_ref, gate_ref, cos_ref, sin_ref, gng_ref, gnb_ref,
                      o_ref, state_ref):
    c = RET_CHUNK
    d = RET_HEAD_DIM

    @pl.when(pl.program_id(1) == 0)
    def _():
        state_ref[...] = jnp.zeros_like(state_ref)

    cos = cos_ref[...]
    sin = sin_ref[...]
    i_row = lax.broadcasted_iota(jnp.int32, (c, c), 0)
    i_col = lax.broadcasted_iota(jnp.int32, (c, c), 1)
    rel = (i_row - i_col).astype(F32)
    pos = lax.broadcasted_iota(jnp.int32, (c, 1), 0).astype(F32)
    for h in range(RET_HEADS):
        log_g = math.log(1.0 - 2.0 ** (-5.0 - h))
        sl = slice(h * d, (h + 1) * d)
        q = q_ref[0, :, sl]
        k = k_ref[0, :, sl]
        q = q * cos + pltpu.roll(q, d // 2, 1) * sin
        k = (k * cos + pltpu.roll(k, d // 2, 1) * sin) * (d ** -0.5)
        vb = v_ref[0, :, sl].astype(BF16)
        decay = jnp.where(rel >= 0, jnp.exp(jnp.maximum(rel, 0.0) * log_g), 0.0)
        inner = lax.dot_general(q.astype(BF16), k.astype(BF16), (((1,), (1,)), ((), ())),
                                preferred_element_type=F32) * decay
        o = jnp.dot(inner.astype(BF16), vb, preferred_element_type=F32)
        state = state_ref[h]
        xi = jnp.exp((pos + 1.0) * log_g)
        o = o + jnp.dot((q * xi).astype(BF16), state.astype(BF16), preferred_element_type=F32)
        zeta = jnp.exp((c - 1.0 - pos) * log_g)
        kv = jnp.dot((k * zeta).T.astype(BF16), vb, preferred_element_type=F32)
        state_ref[h] = math.exp(c * log_g) * state + kv
        mu = jnp.mean(o, axis=-1, keepdims=True)
        oc = o - mu
        var = jnp.mean(oc * oc, axis=-1, keepdims=True)
        y = oc * lax.rsqrt(var + LN_EPS) * gng_ref[:, sl] + gnb_ref[:, sl]
        gate = gate_ref[0, :, sl]
        o_ref[0, :, sl] = gate * _sigmoid(gate) * y


def _rotary_tables(t):
    d = RET_HEAD_DIM
    freqs = ROPE_BASE ** (-jnp.arange(0, d, 2, dtype=F32) / d)
    ang = jnp.arange(t, dtype=F32)[:, None] * freqs[None, :]
    cos, sin = jnp.cos(ang), jnp.sin(ang)
    return jnp.concatenate([cos, cos], axis=1), jnp.concatenate([-sin, sin], axis=1)


def _retention(ret, gn_g, gn_b):
    bsz, t, _ = ret.shape
    c = RET_CHUNK
    cos2, sin2 = _rotary_tables(t)
    part = lambda j: pl.BlockSpec((1, c, RET_WIDTH), lambda b, n: (b, n, j))
    tab = pl.BlockSpec((c, RET_HEAD_DIM), lambda b, n: (n, 0))
    return pl.pallas_call(
        _retention_kernel,
        out_shape=jax.ShapeDtypeStruct((bsz, t, RET_WIDTH), F32),
        grid=(bsz, t // c),
        in_specs=[part(0), part(1), part(2), part(3), tab, tab,
                  _resident((1, RET_WIDTH)), _resident((1, RET_WIDTH))],
        out_specs=pl.BlockSpec((1, c, RET_WIDTH), lambda b, n: (b, n, 0)),
        scratch_shapes=[pltpu.VMEM((RET_HEADS, RET_HEAD_DIM, RET_HEAD_DIM), F32)],
        compiler_params=_params("parallel", "arbitrary"),
        name="retention",
    )(ret, ret, ret, ret, cos2, sin2, gn_g.reshape(1, -1), gn_b.reshape(1, -1))


def _out_ln_kernel(x_ref, on_ref, or_ref, wn_ref, wr_ref, g_ref, b_ref, o_ref):
    mix = (jnp.dot(on_ref[...].astype(BF16), wn_ref[...], preferred_element_type=F32)
           + jnp.dot(or_ref[...].astype(BF16), wr_ref[...], preferred_element_type=F32))
    o_ref[...] = _layer_norm(ALPHA * x_ref[...] + mix, g_ref[...], b_ref[...])


def _out_ln(x2d, o_nsa, o_ret, w_out, g, b):
    n = x2d.shape[0]
    row = lambda w: pl.BlockSpec((ROW_TILE, w), lambda i: (i, 0))
    return pl.pallas_call(
        _out_ln_kernel,
        out_shape=jax.ShapeDtypeStruct((n, D_MODEL), F32),
        grid=(n // ROW_TILE,),
        in_specs=[row(D_MODEL), row(NSA_WIDTH), row(RET_WIDTH),
                  _resident((NSA_WIDTH, D_MODEL)), _resident((RET_WIDTH, D_MODEL)),
                  _resident((1, D_MODEL)), _resident((1, D_MODEL))],
        out_specs=row(D_MODEL),
        compiler_params=_params("parallel"),
        name="out_ln",
    )(x2d, o_nsa, o_ret, w_out[:NSA_WIDTH].astype(BF16), w_out[NSA_WIDTH:].astype(BF16),
      g.reshape(1, -1), b.reshape(1, -1))


def _ple_ln_kernel(x_ref, p_ref, wp_ref, wg_ref, g_ref, b_ref, o_ref):
    x = x_ref[...]
    e = (jnp.dot(p_ref[...].astype(BF16), wp_ref[...], preferred_element_type=F32)
         * _sigmoid(jnp.dot(x.astype(BF16), wg_ref[...], preferred_element_type=F32)))
    o_ref[...] = _layer_norm(ALPHA * x + e, g_ref[...], b_ref[...])


def _ple_ln(x2d, p2d, w_ple, w_gate, g, b):
    n = x2d.shape[0]
    row = lambda w: pl.BlockSpec((ROW_TILE, w), lambda i: (i, 0))
    return pl.pallas_call(
        _ple_ln_kernel,
        out_shape=jax.ShapeDtypeStruct((n, D_MODEL), F32),
        grid=(n // ROW_TILE,),
        in_specs=[row(D_MODEL), row(PLE_DIM), _resident((PLE_DIM, D_MODEL)),
                  _resident((D_MODEL, D_MODEL)), _resident((1, D_MODEL)), _resident((1, D_MODEL))],
        out_specs=row(D_MODEL),
        compiler_params=_params("parallel"),
        name="ple_ln",
    )(x2d, p2d, w_ple.astype(BF16), w_gate.astype(BF16), g.reshape(1, -1), b.reshape(1, -1))


def _token_mixer(x, w_in, cmp_pos, cmp_w1, cmp_w2, gn_g, gn_b):
    bsz, t, _ = x.shape
    g, dh = NSA_GROUPS, NSA_HEAD_DIM
    cmp_raw, ret, k_aug, kw, qt, vs_t, vw_t, gt = _in_proj(x, w_in)
    halves = cmp_raw.reshape(bsz, t, 2, g, dh).transpose(2, 0, 3, 1, 4).reshape(
        2, bsz, g, t // CMP_STRIDE, CMP_STRIDE * dh)
    kc, vct = _compress(halves[0], halves[1], cmp_pos, cmp_w1, cmp_w2)
    oct_, nsel = _nsa_select(qt, kc, vct)
    o_nsa = _nsa_main(qt, nsel, k_aug, kw, vs_t, vw_t, oct_, gt, min(512, t))
    o_ret = _retention(ret, gn_g, gn_b)
    return o_nsa, o_ret


def kernel(x, p, ffn_w13, ffn_w2, w_in, cmp_pos, cmp_w1, cmp_w2, ret_gn_g, ret_gn_b,
           w_out, w_ple, w_ple_gate, ln_g, ln_b):
    bsz, t, d = x.shape
    n = bsz * t
    w13, w2 = ffn_w13.astype(BF16), ffn_w2.astype(BF16)
    for i in range(ffn_w13.shape[0]):
        h = _ffn_ln(x.reshape(n, d), w13[i, 0], w2[i, 0], ln_g[i, 0], ln_b[i, 0])
        o_nsa, o_ret = _token_mixer(h.reshape(bsz, t, d), w_in[i], cmp_pos[i], cmp_w1[i], cmp_w2[i],
                                    ret_gn_g[i], ret_gn_b[i])
        h = _out_ln(h, o_nsa.reshape(n, -1), o_ret.reshape(n, -1), w_out[i], ln_g[i, 1], ln_b[i, 1])
        h = _ffn_ln(h, w13[i, 1], w2[i, 1], ln_g[i, 2], ln_b[i, 2])
        h = _ple_ln(h, p[i].reshape(n, -1), w_ple[i], w_ple_gate[i], ln_g[i, 3], ln_b[i, 3])
        x = h.reshape(bsz, t, d)
    return x
```

```python
import functools
import math

import jax
import jax.numpy as jnp
import numpy as np
from jax import lax
from jax.experimental import pallas as pl
from jax.experimental.pallas import tpu as pltpu

F32 = jnp.float32
BF16 = jnp.bfloat16

D_MODEL = 1024
D_FF = 2816
NSA_GROUPS = 2
NSA_REP = 4
NSA_HEAD_DIM = 64
N_BRANCH = 3
CMP_BLOCK = 32
CMP_STRIDE = 16
CMP_HIDDEN = 256
SLC_BLOCK = 64
SLC_TOPK = 16
WINDOW = 512
Q_BLOCK = 128
RET_HEADS = 4
RET_HEAD_DIM = 128
RET_CHUNK = 128
ROPE_BASE = 10000.0
PLE_DIM = 256
DEPTH = 1
ALPHA = (2.0 * DEPTH) ** 0.25
LN_EPS = 1e-5
NEG = -1e30
LOG2_E = math.log2(math.e)

NSA_WIDTH = NSA_GROUPS * NSA_REP * NSA_HEAD_DIM
RET_WIDTH = RET_HEADS * RET_HEAD_DIM
GROUP_Q = NSA_REP * NSA_HEAD_DIM
LANES_Q = NSA_REP * Q_BLOCK
GATE_ROWS = 16
KV_WIDTH = NSA_GROUPS * NSA_HEAD_DIM
TR_Q, TR_VS, TR_VW, TR_G = 0, 512, 640, 768
TR_ROWS = TR_G + NSA_GROUPS * GATE_ROWS
NAT_CMP, NAT_RET = 0, 2 * KV_WIDTH
NAT_KS = NAT_RET + 4 * RET_WIDTH
NAT_KW = NAT_KS + NSA_GROUPS * Q_BLOCK
NAT_COLS = NAT_KW + NSA_GROUPS * Q_BLOCK

KEY_CHUNKS = 1
SUM_ROWS = 16
SEL_BLOCKS = 4
VMEM_LIMIT = 56 * 1024 * 1024
ROW_TILE = 512


def _layer_norm(z, g, b):
    mu = jnp.mean(z, axis=-1, keepdims=True)
    zc = z - mu
    var = jnp.mean(zc * zc, axis=-1, keepdims=True)
    return zc * lax.rsqrt(var + LN_EPS) * g + b


def _sigmoid(v):
    return 1.0 / (1.0 + jnp.exp(-v))


def _params(*sem):
    return pltpu.CompilerParams(dimension_semantics=sem, vmem_limit_bytes=VMEM_LIMIT)


def _resident(shape):
    nd = len(shape)
    return pl.BlockSpec(shape, lambda *_: (0,) * nd, pipeline_mode=pl.Buffered(1))


def _swiglu(xb, w13_ref, w2_ref):
    a = jnp.dot(xb, w13_ref[0, 0, :, :D_FF], preferred_element_type=F32)
    u = jnp.dot(xb, w13_ref[0, 0, :, D_FF:], preferred_element_type=F32)
    h = (a * _sigmoid(a) * u).astype(BF16)
    return jnp.dot(h, w2_ref[0, 0], preferred_element_type=F32)


def _ffn_weight_specs(layer, which):
    pick = lambda *_: (layer, which, 0, 0)
    return [pl.BlockSpec((1, 1, D_MODEL, 2 * D_FF), pick, pipeline_mode=pl.Buffered(1)),
            pl.BlockSpec((1, 1, D_FF, D_MODEL), pick, pipeline_mode=pl.Buffered(1))]


def _ffn_ln_kernel(x_ref, w13_ref, w2_ref, g_ref, b_ref, o_ref):
    x = x_ref[...]
    o_ref[...] = _layer_norm(ALPHA * x + 0.5 * _swiglu(x.astype(BF16), w13_ref, w2_ref), g_ref[...], b_ref[...])


def _ffn_ln(x2d, w13, w2, layer, which, g, b):
    n = x2d.shape[0]
    row = pl.BlockSpec((ROW_TILE, D_MODEL), lambda i: (i, 0))
    return pl.pallas_call(
        _ffn_ln_kernel,
        out_shape=jax.ShapeDtypeStruct((n, D_MODEL), F32),
        grid=(n // ROW_TILE,),
        in_specs=[row, *_ffn_weight_specs(layer, which), _resident((1, D_MODEL)), _resident((1, D_MODEL))],
        out_specs=row,
        compiler_params=_params("parallel"),
        name="ffn_ln",
    )(x2d, w13, w2, g.reshape(1, -1), b.reshape(1, -1))


def _in_proj_kernel(x_ref, wn_ref, wt_ref, ck_ref, cv_ref, ret_ref, ka_ref, kw_ref, qt_ref, vst_ref, vwt_ref,
                    gt_ref):
    xb = x_ref[0].astype(BF16)
    tm = xb.shape[0]
    nat = jnp.dot(xb, wn_ref[...], preferred_element_type=F32)
    ck_ref[0] = nat[:, NAT_CMP:NAT_CMP + KV_WIDTH]
    cv_ref[0] = nat[:, NAT_CMP + KV_WIDTH:NAT_RET]
    ret_ref[0] = nat[:, NAT_RET:NAT_KS]
    ns = ka_ref.shape[3] - Q_BLOCK
    row = pl.program_id(1) * tm + lax.broadcasted_iota(jnp.int32, (tm, ns), 0)
    own = row // SLC_BLOCK == lax.broadcasted_iota(jnp.int32, (tm, ns), 1)
    bias = jnp.where(own, NEG, 0.0).astype(BF16)
    for g in range(NSA_GROUPS):
        ka_ref[0, g, :, 0:Q_BLOCK] = nat[:, NAT_KS + g * Q_BLOCK:NAT_KS + (g + 1) * Q_BLOCK].astype(BF16)
        ka_ref[0, g, :, Q_BLOCK:] = bias
        kw_ref[0, g] = nat[:, NAT_KW + g * Q_BLOCK:NAT_KW + (g + 1) * Q_BLOCK].astype(BF16)
    tr = lax.dot_general(wt_ref[...], xb, (((1,), (1,)), ((), ())), preferred_element_type=F32)
    qt_ref[0] = (tr[TR_Q:TR_VS] * (NSA_HEAD_DIM ** -0.5 * LOG2_E)).astype(BF16)
    vst_ref[0] = tr[TR_VS:TR_VW].astype(BF16)
    vwt_ref[0] = tr[TR_VW:TR_G].astype(BF16)
    gt_ref[0] = tr[TR_G:]


def _split_w_in(w_in):
    c_q = NSA_WIDTH
    c_kv = c_q + 2 * N_BRANCH * KV_WIDTH
    c_g = c_kv + NSA_GROUPS * NSA_REP * N_BRANCH
    kv = w_in[:, c_q:c_kv].reshape(D_MODEL, 2 * N_BRANCH, KV_WIDTH)
    gates = w_in[:, c_kv:c_g].reshape(D_MODEL, NSA_GROUPS, NSA_REP * N_BRANCH)
    gates = jnp.pad(gates, ((0, 0), (0, 0), (0, GATE_ROWS - NSA_REP * N_BRANCH)))
    pad_groups = lambda w: jnp.pad(w.reshape(D_MODEL, NSA_GROUPS, NSA_HEAD_DIM),
                                   ((0, 0), (0, 0), (0, Q_BLOCK - NSA_HEAD_DIM))).reshape(D_MODEL, -1)
    w_nat = jnp.concatenate([kv[:, 0], kv[:, 1], w_in[:, c_g:], pad_groups(kv[:, 2]), pad_groups(kv[:, 4])],
                            axis=1)
    w_tr = jnp.concatenate([w_in[:, :c_q], kv[:, 3], kv[:, 5],
                            gates.reshape(D_MODEL, NSA_GROUPS * GATE_ROWS)], axis=1)
    return w_nat.astype(BF16), w_tr.T.astype(BF16)


def _in_proj(x, w_in):
    bsz, t, _ = x.shape
    g = NSA_GROUPS
    ns = t // SLC_BLOCK
    w_nat, w_trt = _split_w_in(w_in)
    rows = lambda w: pl.BlockSpec((1, ROW_TILE, w), lambda b, i: (b, i, 0))
    cols = lambda h: pl.BlockSpec((1, h, ROW_TILE), lambda b, i: (b, 0, i))
    grouped = lambda w: pl.BlockSpec((1, g, ROW_TILE, w), lambda b, i: (b, 0, i, 0))
    return pl.pallas_call(
        _in_proj_kernel,
        out_shape=(jax.ShapeDtypeStruct((bsz, t, KV_WIDTH), F32),
                   jax.ShapeDtypeStruct((bsz, t, KV_WIDTH), F32),
                   jax.ShapeDtypeStruct((bsz, t, NAT_KS - NAT_RET), F32),
                   jax.ShapeDtypeStruct((bsz, g, t, Q_BLOCK + ns), BF16),
                   jax.ShapeDtypeStruct((bsz, g, t, Q_BLOCK), BF16),
                   jax.ShapeDtypeStruct((bsz, TR_VS - TR_Q, t), BF16),
                   jax.ShapeDtypeStruct((bsz, TR_VW - TR_VS, t), BF16),
                   jax.ShapeDtypeStruct((bsz, TR_G - TR_VW, t), BF16),
                   jax.ShapeDtypeStruct((bsz, TR_ROWS - TR_G, t), F32)),
        grid=(bsz, t // ROW_TILE),
        in_specs=[rows(D_MODEL), _resident((D_MODEL, NAT_COLS)), _resident((TR_ROWS, D_MODEL))],
        out_specs=(rows(KV_WIDTH), rows(KV_WIDTH), rows(NAT_KS - NAT_RET), grouped(Q_BLOCK + ns), grouped(Q_BLOCK),
                   cols(TR_VS - TR_Q), cols(TR_VW - TR_VS), cols(TR_G - TR_VW), cols(TR_ROWS - TR_G)),
        compiler_params=_params("parallel", "parallel"),
        name="in_proj",
    )(x, w_nat, w_trt)


def _gelu_tanh(v):
    return 0.5 * v * (1.0 + jnp.tanh(math.sqrt(2.0 / math.pi) * (v + 0.044715 * (v * v * v))))


def _compress_hidden(h, pos_ref, w1_ref):
    top = jnp.dot((h + pos_ref[0:1, :]).astype(BF16), w1_ref[0], preferred_element_type=F32)
    bot = jnp.dot((h + pos_ref[1:2, :]).astype(BF16), w1_ref[1], preferred_element_type=F32)
    nc = h.shape[0]
    return _gelu_tanh(top + pltpu.roll(bot, nc - 1, 0)).astype(BF16)


def _compress_kernel(rk_ref, rv_ref, pk_ref, pv_ref, w1k_ref, w1v_ref, w2k_ref, w2vt_ref, kc_ref, vct_ref):
    dh = NSA_HEAD_DIM
    nc = kc_ref.shape[2]

    def half_blocks(raw_ref, g):
        rows = [raw_ref[0, pl.ds(l, nc, stride=CMP_STRIDE), :] for l in range(CMP_STRIDE)]
        return jnp.concatenate([r[:, g * dh:(g + 1) * dh] for r in rows], axis=1)

    row = lax.broadcasted_iota(jnp.int32, (nc, dh), 0)
    col = lax.broadcasted_iota(jnp.int32, (dh, nc), 1)
    for g in range(NSA_GROUPS):
        gk = _compress_hidden(half_blocks(rk_ref, g), pk_ref, w1k_ref)
        kc = jnp.dot(gk, w2k_ref[...], preferred_element_type=F32)
        kc_ref[0, g] = jnp.where(row < nc - 1, kc, 0.0).astype(BF16)
        gv = _compress_hidden(half_blocks(rv_ref, g), pv_ref, w1v_ref)
        vct = lax.dot_general(w2vt_ref[...], gv, (((1,), (1,)), ((), ())),
                              preferred_element_type=F32)
        vct_ref[0, g, 0:dh, :] = jnp.where(col < nc - 1, vct, 0.0).astype(BF16)
        vct_ref[0, g, dh:, :] = jnp.ones((SUM_ROWS, nc), BF16)


def _compress(raw_k, raw_v, cmp_pos, cmp_w1, cmp_w2):
    bsz, t, width = raw_k.shape
    g, dh = NSA_GROUPS, NSA_HEAD_DIM
    nc = t // CMP_STRIDE
    half = CMP_BLOCK // 2
    hw = half * dh
    pos = cmp_pos.reshape(2, 2, hw)
    w1 = cmp_w1.reshape(2, 2, hw, CMP_HIDDEN).astype(BF16)
    return pl.pallas_call(
        _compress_kernel,
        out_shape=(jax.ShapeDtypeStruct((bsz, g, nc, dh), BF16),
                   jax.ShapeDtypeStruct((bsz, g, dh + SUM_ROWS, nc), BF16)),
        grid=(bsz,),
        in_specs=[pl.BlockSpec((1, t, width), lambda b: (b, 0, 0)), pl.BlockSpec((1, t, width), lambda b: (b, 0, 0)),
                  _resident((2, hw)), _resident((2, hw)),
                  _resident((2, hw, CMP_HIDDEN)), _resident((2, hw, CMP_HIDDEN)),
                  _resident((CMP_HIDDEN, dh)), _resident((dh, CMP_HIDDEN))],
        out_specs=(pl.BlockSpec((1, g, nc, dh), lambda b: (b, 0, 0, 0)),
                   pl.BlockSpec((1, g, dh + SUM_ROWS, nc), lambda b: (b, 0, 0, 0))),
        compiler_params=_params("parallel"),
        name="compress",
    )(raw_k, raw_v, pos[0], pos[1], w1[0], w1[1], cmp_w2[0].astype(BF16), cmp_w2[1].T.astype(BF16))


def _q_lanes(qt_ref, j=0):
    qt = qt_ref[0, :, j * Q_BLOCK:(j + 1) * Q_BLOCK]
    return jnp.concatenate([qt[r * NSA_HEAD_DIM:(r + 1) * NSA_HEAD_DIM, :] for r in range(NSA_REP)], axis=1)


def _lane_token(shape, q0):
    return q0 + (lax.broadcasted_iota(jnp.int32, shape, 1) & (Q_BLOCK - 1))


def _nsa_select_kernel(qt_ref, kc_ref, vct_ref, ovl_ref, bias_ref, oct_ref, nsel_ref, imp_ref):
    step = pl.program_id(2)
    dh = NSA_HEAD_DIM
    nc = kc_ref.shape[2]
    kc = kc_ref[0, 0]
    lhs = jnp.concatenate([ovl_ref[...], vct_ref[0, 0]], axis=0)
    n_blk = ovl_ref.shape[0]

    def scores(j):
        qb = step * SEL_BLOCKS + j
        s = jnp.dot(kc, _q_lanes(qt_ref, j), preferred_element_type=F32)
        bias = bias_ref[pl.ds(pl.multiple_of(nc - qb * (Q_BLOCK // CMP_STRIDE), 8), nc), :]
        return s + jnp.concatenate([bias] * NSA_REP, axis=1)

    def select(pair):
        lanes = slice(pair * 2 * Q_BLOCK, (pair + 1) * 2 * Q_BLOCK)
        imp = imp_ref[:, lanes]
        blk = lax.broadcasted_iota(jnp.int32, imp.shape, 0)
        t = ((step * SEL_BLOCKS + 2 * pair) * Q_BLOCK
             + lax.broadcasted_iota(jnp.int32, imp.shape, 1))
        cur = t // SLC_BLOCK
        valid = blk <= cur
        forced = (blk == 0) | (blk >= cur - 1)
        val0 = jnp.where(valid & jnp.logical_not(forced), imp, -jnp.inf)
        blk_f = blk.astype(F32)
        val = val0
        for _ in range(max(min(SLC_TOPK, n_blk) - 3, 0)):
            top = jnp.max(val, axis=0, keepdims=True)
            first = jnp.min(jnp.where(val == top, blk_f, float(n_blk)), axis=0, keepdims=True)
            val = jnp.where(blk_f == first, -jnp.inf, val)
            yield
        chosen = valid & (forced | (val != val0))
        nsel_ref[0, 0, :, lanes] = jnp.where(chosen & (blk < 2 * (t // Q_BLOCK)), 0.0, 1.0).astype(BF16)

    running = []

    def advance(n):
        for _ in range(n):
            if running and next(running[0], True):
                running.pop(0)

    s_next = scores(0)
    for j in range(SEL_BLOCKS):
        qb = step * SEL_BLOCKS + j
        s = s_next
        if j + 1 < SEL_BLOCKS:
            s_next = scores(j + 1)
        advance(3)
        e = jnp.exp2(s - jnp.max(s, axis=0, keepdims=True)).astype(BF16)
        advance(3)
        r = jnp.dot(lhs, e, preferred_element_type=F32)
        tok = _lane_token((1, LANES_Q), qb * Q_BLOCK)
        scale = jnp.where(tok >= CMP_BLOCK - 1, 1.0 / r[n_blk + dh:n_blk + dh + 1], 0.0)
        o = r[n_blk:n_blk + dh] * scale
        w = r[0:n_blk] * scale
        imp = w[:, 0:Q_BLOCK]
        for h in range(NSA_REP):
            oct_ref[0, h, :, j * Q_BLOCK:(j + 1) * Q_BLOCK] = o[:, h * Q_BLOCK:(h + 1) * Q_BLOCK]
            if h:
                imp = imp + w[:, h * Q_BLOCK:(h + 1) * Q_BLOCK]
        imp_ref[:, j * Q_BLOCK:(j + 1) * Q_BLOCK] = imp
        advance(3)
        if j % 2 == 1:
            running.append(select(j // 2))
    while running:
        advance(1)


def _cmp_bias_table(t):
    nc = t // CMP_STRIDE
    rel = (np.arange(2 * nc)[:, None] - nc) * CMP_STRIDE + CMP_BLOCK - 1
    return jnp.asarray(np.where(rel <= np.arange(Q_BLOCK)[None, :], 0.0, NEG), dtype=F32)


def _overlap_t(t):
    n_cmp = t // CMP_STRIDE
    n_slc = t // SLC_BLOCK
    cs = np.arange(n_cmp) * CMP_STRIDE
    ss = np.arange(n_slc) * SLC_BLOCK
    ov = np.clip(np.minimum(cs[None, :] + CMP_BLOCK, ss[:, None] + SLC_BLOCK)
                 - np.maximum(cs[None, :], ss[:, None]), 0, None) / CMP_BLOCK
    return jnp.asarray(ov, dtype=BF16)


def _nsa_select(qt, kc, vct):
    bsz, _, t = qt.shape
    g, nc, dh = kc.shape[1:]
    ns = t // SLC_BLOCK
    sq = SEL_BLOCKS * Q_BLOCK
    assert ns % 16 == 0 and t % sq == 0 and SEL_BLOCKS % 2 == 0
    return pl.pallas_call(
        _nsa_select_kernel,
        out_shape=(jax.ShapeDtypeStruct((bsz, g * NSA_REP, dh, t), F32),
                   jax.ShapeDtypeStruct((bsz, g, ns, t), BF16)),
        grid=(bsz, g, t // sq),
        in_specs=[pl.BlockSpec((1, GROUP_Q, sq), lambda b, j, i: (b, j, i)),
                  pl.BlockSpec((1, 1, nc, dh), lambda b, j, i: (b, j, 0, 0)),
                  pl.BlockSpec((1, 1, dh + SUM_ROWS, nc), lambda b, j, i: (b, j, 0, 0)),
                  _resident((ns, nc)), _resident((2 * nc, Q_BLOCK))],
        out_specs=(pl.BlockSpec((1, NSA_REP, dh, sq), lambda b, j, i: (b, j, 0, i)),
                   pl.BlockSpec((1, 1, ns, sq), lambda b, j, i: (b, j, 0, i))),
        scratch_shapes=[pltpu.VMEM((ns, sq), F32)],
        compiler_params=_params("parallel", "parallel", "parallel"),
        name="nsa_select",
    )(qt, kc, vct, _overlap_t(t), _cmp_bias_table(t))


def _nsa_main_kernel(qt_ref, nsel_ref, ka_ref, vst_ref, kw_ref, vwt_ref, oct_ref, gt_ref, band_ref,
                     o_ref, s0_ref, s1_ref, p0_ref, p1_ref, acc_ref, ow_ref, *, key_tile):
    qb = pl.program_id(2)
    q0 = qb * Q_BLOCK
    dh = NSA_HEAD_DIM
    last_tile = ka_ref.shape[2] // key_tile - 1
    qm = _q_lanes(qt_ref)
    nsel = nsel_ref[0, 0]
    q_pad = jnp.concatenate([qm, jnp.zeros((Q_BLOCK - dh, LANES_Q), BF16)], axis=0)
    q_aug = jnp.concatenate([q_pad, jnp.concatenate([nsel] * NSA_REP, axis=1)], axis=0)

    half = LANES_Q // 2
    chunk = key_tile // KEY_CHUNKS
    q_half = [q_aug[:, h * half:(h + 1) * half] for h in range(2)]
    pieces = [(h, c) for c in range(KEY_CHUNKS) for h in range(2)]

    def step(s_in, p_out, col_max, m, l, kt_v, p_in, a_v, kt_n, s_out):
        m_new = jnp.maximum(m, col_max)
        a = jnp.exp2(m - m_new)
        kv0 = pl.multiple_of(kt_v * key_tile, key_tile)
        kn0 = pl.multiple_of(kt_n * key_tile, key_tile)
        sums = [None, None]
        new_max = [None, None]
        acc = [acc_ref[:, h * half:(h + 1) * half] * a_v[:, h * half:(h + 1) * half] for h in range(2)]
        for h, c in pieces:
            lanes = slice(h * half, (h + 1) * half)
            rows = slice(c * chunk, (c + 1) * chunk)
            p = jnp.exp2(s_in[rows, lanes] - m_new[:, lanes])
            p_out[rows, lanes] = p.astype(BF16)
            ps = jnp.sum(p, axis=0, keepdims=True)
            sums[h] = ps if sums[h] is None else sums[h] + ps
            acc[h] = acc[h] + jnp.dot(vst_ref[0, :, pl.ds(kv0 + c * chunk, chunk)], p_in[rows, lanes],
                                      preferred_element_type=F32)
            s = jnp.dot(ka_ref[0, 0, pl.ds(kn0 + c * chunk, chunk), :], q_half[h],
                        preferred_element_type=F32)
            s_out[rows, lanes] = s
            cm = jnp.max(s, axis=0, keepdims=True)
            new_max[h] = cm if new_max[h] is None else jnp.maximum(new_max[h], cm)
        for h in range(2):
            acc_ref[:, h * half:(h + 1) * half] = acc[h]
        return (m_new, l * a + jnp.concatenate(sums, axis=1), a, jnp.concatenate(new_max, axis=1))

    def pair(j, carry):
        m, l, a_prev, cm0 = carry
        t0 = 2 * j
        m, l, a0, cm1 = step(s0_ref, p0_ref, cm0, m, l,
                             jnp.maximum(t0 - 1, 0), p1_ref, a_prev, t0 + 1, s1_ref)
        m, l, a1, cm0 = step(s1_ref, p1_ref, cm1, m, l,
                             t0, p0_ref, a0, jnp.minimum(t0 + 2, last_tile), s0_ref)
        return m, l, a1, cm0

    def scores(kt, s_ref):
        k0 = pl.multiple_of(kt * key_tile, key_tile)
        s = jnp.dot(ka_ref[0, 0, pl.ds(k0, key_tile), :], q_aug, preferred_element_type=F32)
        s_ref[...] = s
        return jnp.max(s, axis=0, keepdims=True)

    def values(kt, p_ref, a):
        k0 = pl.multiple_of(kt * key_tile, key_tile)
        acc_ref[...] = acc_ref[...] * a + jnp.dot(vst_ref[0, :, pl.ds(k0, key_tile)], p_ref[...],
                                                  preferred_element_type=F32)

    p1_ref[...] = jnp.zeros_like(p1_ref)
    cm_first = scores(0, s0_ref)

    d0 = pl.multiple_of(q0, Q_BLOCK)
    s = jnp.dot(ka_ref[0, 0, pl.ds(d0, Q_BLOCK), 0:Q_BLOCK], q_pad, preferred_element_type=F32)
    tq = lax.broadcasted_iota(jnp.int32, s.shape, 1) & (Q_BLOCK - 1)
    kl = lax.broadcasted_iota(jnp.int32, s.shape, 0)
    s = jnp.where(kl <= tq, s, NEG)
    m_own = jnp.max(s, axis=0, keepdims=True)
    p = jnp.exp2(s - m_own)
    acc_ref[...] = jnp.dot(vst_ref[0, :, pl.ds(d0, Q_BLOCK)], p.astype(BF16),
                           preferred_element_type=F32)

    wlen = WINDOW + Q_BLOCK
    w0 = pl.multiple_of(jnp.maximum(q0 - WINDOW, 0), Q_BLOCK)
    band = band_ref[pl.ds(pl.multiple_of(WINDOW - (q0 - w0), Q_BLOCK), wlen), :]
    sw = (jnp.dot(kw_ref[0, 0, pl.ds(w0, wlen), :], q_pad, preferred_element_type=F32)
          + jnp.concatenate([band] * NSA_REP, axis=1))
    e = jnp.exp2(sw - jnp.max(sw, axis=0, keepdims=True))
    ow_ref[...] = (jnp.dot(vwt_ref[0, :, pl.ds(w0, wlen)], e.astype(BF16), preferred_element_type=F32)
                   / jnp.sum(e, axis=0, keepdims=True))

    n_pairs = ((q0 + key_tile - 1) // key_tile + 1) // 2
    init = (m_own, jnp.sum(p, axis=0, keepdims=True), jnp.ones((1, LANES_Q), F32), cm_first)
    _, l, a_last, _ = lax.fori_loop(0, n_pairs, pair, init)
    values(jnp.maximum(2 * n_pairs - 1, 0), p1_ref, a_last)
    o_s = acc_ref[...] / l
    o_w = ow_ref[...]

    gates = _sigmoid(gt_ref[0])
    outs = []
    for r in range(NSA_REP):
        sl = slice(r * Q_BLOCK, (r + 1) * Q_BLOCK)
        g_c, g_s, g_w = (gates[r * N_BRANCH + i:r * N_BRANCH + i + 1, :] for i in range(N_BRANCH))
        outs.append(g_c * oct_ref[0, r] + g_s * o_s[:, sl] + g_w * o_w[:, sl])
    o_ref[0] = jnp.concatenate(outs, axis=0).T.astype(BF16)


def _nsa_main(qt, nsel, k_aug, kw, vs_t, vw_t, oct_, gt, key_tile):
    bsz, _, t = qt.shape
    g = k_aug.shape[1]
    dh = NSA_HEAD_DIM
    ns = t // SLC_BLOCK
    nq = t // Q_BLOCK
    aw = k_aug.shape[-1]
    assert (t // key_tile) % 2 == 0 and key_tile % Q_BLOCK == 0 and t >= WINDOW + Q_BLOCK
    rel = np.arange(2 * WINDOW + Q_BLOCK)[:, None] - WINDOW - np.arange(Q_BLOCK)[None, :]
    band = jnp.asarray(np.where((rel <= 0) & (rel > -WINDOW), 0.0, NEG), dtype=F32)
    return pl.pallas_call(
        functools.partial(_nsa_main_kernel, key_tile=key_tile),
        out_shape=jax.ShapeDtypeStruct((bsz, t, NSA_WIDTH), BF16),
        grid=(bsz, g, nq),
        in_specs=[pl.BlockSpec((1, GROUP_Q, Q_BLOCK), lambda b, j, i: (b, j, i)),
                  pl.BlockSpec((1, 1, ns, Q_BLOCK), lambda b, j, i: (b, j, 0, i)),
                  pl.BlockSpec((1, 1, t, aw), lambda b, j, i: (b, j, 0, 0)),
                  pl.BlockSpec((1, dh, t), lambda b, j, i: (b, j, 0)),
                  pl.BlockSpec((1, 1, t, Q_BLOCK), lambda b, j, i: (b, j, 0, 0)),
                  pl.BlockSpec((1, dh, t), lambda b, j, i: (b, j, 0)),
                  pl.BlockSpec((1, NSA_REP, dh, Q_BLOCK), lambda b, j, i: (b, j, 0, i)),
                  pl.BlockSpec((1, GATE_ROWS, Q_BLOCK), lambda b, j, i: (b, j, i)),
                  _resident(band.shape)],
        out_specs=pl.BlockSpec((1, Q_BLOCK, GROUP_Q), lambda b, j, i: (b, i, j)),
        scratch_shapes=[pltpu.VMEM((key_tile, LANES_Q), F32), pltpu.VMEM((key_tile, LANES_Q), F32),
                        pltpu.VMEM((key_tile, LANES_Q), BF16), pltpu.VMEM((key_tile, LANES_Q), BF16),
                        pltpu.VMEM((dh, LANES_Q), F32), pltpu.VMEM((dh, LANES_Q), F32)],
        compiler_params=_params("parallel", "parallel", "arbitrary"),
        name="nsa_main",
    )(qt, nsel, k_aug, vs_t, kw, vw_t, oct_, gt, band)


def _retention_kernel(q_ref, k_ref, v_ref, gate_ref, cos_ref, sin_ref, gng_ref, gnb_ref,
                      o_ref, state_ref):
    c = RET_CHUNK
    d = RET_HEAD_DIM

    @pl.when(pl.program_id(1) == 0)
    def _():
        state_ref[...] = jnp.zeros_like(state_ref)

    cos = cos_ref[...]
    sin = sin_ref[...]
    i_row = lax.broadcasted_iota(jnp.int32, (c, c), 0)
    i_col = lax.broadcasted_iota(jnp.int32, (c, c), 1)
    rel = (i_row - i_col).astype(F32)
    pos = lax.broadcasted_iota(jnp.int32, (c, 1), 0).astype(F32)
    for h in range(RET_HEADS):
        log_g = math.log(1.0 - 2.0 ** (-5.0 - h))
        sl = slice(h * d, (h + 1) * d)
        q = q_ref[0, :, sl]
        k = k_ref[0, :, sl]
        q = q * cos + pltpu.roll(q, d // 2, 1) * sin
        k = (k * cos + pltpu.roll(k, d // 2, 1) * sin) * (d ** -0.5)
        vb = v_ref[0, :, sl].astype(BF16)
        decay = jnp.where(rel >= 0, jnp.exp(jnp.maximum(rel, 0.0) * log_g), 0.0)
        inner = lax.dot_general(q.astype(BF16), k.astype(BF16), (((1,), (1,)), ((), ())),
                                preferred_element_type=F32) * decay
        o = jnp.dot(inner.astype(BF16), vb, preferred_element_type=F32)
        state = state_ref[h]
        xi = jnp.exp((pos + 1.0) * log_g)
        o = o + jnp.dot((q * xi).astype(BF16), state.astype(BF16), preferred_element_type=F32)
        zeta = jnp.exp((c - 1.0 - pos) * log_g)
        kv = jnp.dot((k * zeta).T.astype(BF16), vb, preferred_element_type=F32)
        state_ref[h] = math.exp(c * log_g) * state + kv
        mu = jnp.mean(o, axis=-1, keepdims=True)
        oc = o - mu
        var = jnp.mean(oc * oc, axis=-1, keepdims=True)
        y = oc * lax.rsqrt(var + LN_EPS) * gng_ref[:, sl] + gnb_ref[:, sl]
        gate = gate_ref[0, :, sl]
        o_ref[0, :, sl] = (gate * _sigmoid(gate) * y).astype(BF16)


def _rotary_tables(t):
    d = RET_HEAD_DIM
    ang = np.arange(t)[:, None] * ROPE_BASE ** (-np.arange(0, d, 2) / d)[None, :]
    cos, sin = np.cos(ang), np.sin(ang)
    return (jnp.asarray(np.concatenate([cos, cos], axis=1), dtype=F32),
            jnp.asarray(np.concatenate([-sin, sin], axis=1), dtype=F32))


def _retention(ret, gn_g, gn_b):
    bsz, t, _ = ret.shape
    c = RET_CHUNK
    cos2, sin2 = _rotary_tables(t)
    part = lambda j: pl.BlockSpec((1, c, RET_WIDTH), lambda b, n: (b, n, j))
    tab = pl.BlockSpec((c, RET_HEAD_DIM), lambda b, n: (n, 0))
    return pl.pallas_call(
        _retention_kernel,
        out_shape=jax.ShapeDtypeStruct((bsz, t, RET_WIDTH), BF16),
        grid=(bsz, t // c),
        in_specs=[part(0), part(1), part(2), part(3), tab, tab,
                  _resident((1, RET_WIDTH)), _resident((1, RET_WIDTH))],
        out_specs=pl.BlockSpec((1, c, RET_WIDTH), lambda b, n: (b, n, 0)),
        scratch_shapes=[pltpu.VMEM((RET_HEADS, RET_HEAD_DIM, RET_HEAD_DIM), F32)],
        compiler_params=_params("parallel", "arbitrary"),
        name="retention",
    )(ret, ret, ret, ret, cos2, sin2, gn_g.reshape(1, -1), gn_b.reshape(1, -1))


def _tail_kernel(x_ref, on_ref, or_ref, p_ref, wn_ref, wr_ref, w13_ref, w2_ref, wp_ref, wg_ref,
                 g_ref, b_ref, o_ref):
    mix = (jnp.dot(on_ref[...], wn_ref[...], preferred_element_type=F32)
           + jnp.dot(or_ref[...], wr_ref[...], preferred_element_type=F32))
    x = _layer_norm(ALPHA * x_ref[...] + mix, g_ref[0:1, :], b_ref[0:1, :])
    x = _layer_norm(ALPHA * x + 0.5 * _swiglu(x.astype(BF16), w13_ref, w2_ref), g_ref[1:2, :], b_ref[1:2, :])
    e = (jnp.dot(p_ref[...].astype(BF16), wp_ref[...], preferred_element_type=F32)
         * _sigmoid(jnp.dot(x.astype(BF16), wg_ref[...], preferred_element_type=F32)))
    o_ref[...] = _layer_norm(ALPHA * x + e, g_ref[2:3, :], b_ref[2:3, :])


def _tail(x2d, o_nsa, o_ret, p2d, w_out, w13, w2, layer, w_ple, w_gate, g, b):
    n = x2d.shape[0]
    row = lambda w: pl.BlockSpec((ROW_TILE, w), lambda i: (i, 0))
    return pl.pallas_call(
        _tail_kernel,
        out_shape=jax.ShapeDtypeStruct((n, D_MODEL), F32),
        grid=(n // ROW_TILE,),
        in_specs=[row(D_MODEL), row(NSA_WIDTH), row(RET_WIDTH), row(PLE_DIM),
                  _resident((NSA_WIDTH, D_MODEL)), _resident((RET_WIDTH, D_MODEL)),
                  *_ffn_weight_specs(layer, 1),
                  _resident((PLE_DIM, D_MODEL)), _resident((D_MODEL, D_MODEL)),
                  _resident((3, D_MODEL)), _resident((3, D_MODEL))],
        out_specs=row(D_MODEL),
        compiler_params=_params("parallel"),
        name="tail",
    )(x2d, o_nsa, o_ret, p2d, w_out[:NSA_WIDTH].astype(BF16), w_out[NSA_WIDTH:].astype(BF16),
      w13, w2, w_ple.astype(BF16), w_gate.astype(BF16), g, b)


def _token_mixer(x, w_in, cmp_pos, cmp_w1, cmp_w2, gn_g, gn_b):
    t = x.shape[1]
    raw_k, raw_v, ret, k_aug, kw, qt, vs_t, vw_t, gt = _in_proj(x, w_in)
    kc, vct = _compress(raw_k, raw_v, cmp_pos, cmp_w1, cmp_w2)
    oct_, nsel = _nsa_select(qt, kc, vct)
    o_nsa = _nsa_main(qt, nsel, k_aug, kw, vs_t, vw_t, oct_, gt, min(512, t))
    o_ret = _retention(ret, gn_g, gn_b)
    return o_nsa, o_ret


def kernel(x, p, ffn_w13, ffn_w2, w_in, cmp_pos, cmp_w1, cmp_w2, ret_gn_g, ret_gn_b,
           w_out, w_ple, w_ple_gate, ln_g, ln_b):
    bsz, t, d = x.shape
    n = bsz * t
    w13, w2 = ffn_w13.astype(BF16), ffn_w2.astype(BF16)
    for i in range(ffn_w13.shape[0]):
        h = _ffn_ln(x.reshape(n, d), w13, w2, i, 0, ln_g[i, 0], ln_b[i, 0])
        o_nsa, o_ret = _token_mixer(h.reshape(bsz, t, d), w_in[i], cmp_pos[i], cmp_w1[i], cmp_w2[i],
                                    ret_gn_g[i], ret_gn_b[i])
        h = _tail(h, o_nsa.reshape(n, -1), o_ret.reshape(n, -1), p[i].reshape(n, -1), w_out[i],
                  w13, w2, i, w_ple[i], w_ple_gate[i], ln_g[i, 1:], ln_b[i, 1:])
        x = h.reshape(bsz, t, d)
    return x
```

```python
import functools
import math

import jax
import jax.numpy as jnp
import numpy as np
from jax import lax
from jax.experimental import pallas as pl
from jax.experimental.pallas import tpu as pltpu

F32 = jnp.float32
BF16 = jnp.bfloat16

D_MODEL = 1024
D_FF = 2816
NSA_GROUPS = 2
NSA_REP = 4
NSA_HEAD_DIM = 64
N_BRANCH = 3
CMP_BLOCK = 32
CMP_STRIDE = 16
CMP_HIDDEN = 256
SLC_BLOCK = 64
SLC_TOPK = 16
WINDOW = 512
Q_BLOCK = 128
RET_HEADS = 4
RET_HEAD_DIM = 128
RET_CHUNK = 128
RET_STEP = 256
ROPE_BASE = 10000.0
PLE_DIM = 256
DEPTH = 1
ALPHA = (2.0 * DEPTH) ** 0.25
LN_EPS = 1e-5
NEG = -1e30
LOG2_E = math.log2(math.e)

NSA_WIDTH = NSA_GROUPS * NSA_REP * NSA_HEAD_DIM
RET_WIDTH = RET_HEADS * RET_HEAD_DIM
GROUP_Q = NSA_REP * NSA_HEAD_DIM
LANES_Q = NSA_REP * Q_BLOCK
GATE_ROWS = 16
KV_WIDTH = NSA_GROUPS * NSA_HEAD_DIM
TR_Q, TR_VS, TR_VW, TR_G = 0, 512, 640, 768
TR_ROWS = TR_G + NSA_GROUPS * GATE_ROWS
NAT_CMP, NAT_RET = 0, 2 * KV_WIDTH
NAT_KS = NAT_RET + 4 * RET_WIDTH
NAT_KW = NAT_KS + NSA_GROUPS * Q_BLOCK
NAT_COLS = NAT_KW + NSA_GROUPS * Q_BLOCK

KEY_CHUNKS = 1
SUM_ROWS = 16
SEL_BLOCKS = 4
MAIN_BLOCKS = 4
VMEM_LIMIT = 56 * 1024 * 1024
ROW_TILE = 512


def _layer_norm(z, g, b):
    mu = jnp.mean(z, axis=-1, keepdims=True)
    zc = z - mu
    var = jnp.mean(zc * zc, axis=-1, keepdims=True)
    return zc * lax.rsqrt(var + LN_EPS) * g + b


def _sigmoid(v):
    return 1.0 / (1.0 + jnp.exp(-v))


def _params(*sem):
    return pltpu.CompilerParams(dimension_semantics=sem, vmem_limit_bytes=VMEM_LIMIT)


def _resident(shape):
    nd = len(shape)
    return pl.BlockSpec(shape, lambda *_: (0,) * nd, pipeline_mode=pl.Buffered(1))


def _swiglu(xb, w13_ref, w2_ref):
    a = jnp.dot(xb, w13_ref[0, 0, :, :D_FF], preferred_element_type=F32)
    u = jnp.dot(xb, w13_ref[0, 0, :, D_FF:], preferred_element_type=F32)
    h = (a * _sigmoid(a) * u).astype(BF16)
    return jnp.dot(h, w2_ref[0, 0], preferred_element_type=F32)


def _ffn_weight_specs(layer, which):
    pick = lambda *_: (layer, which, 0, 0)
    return [pl.BlockSpec((1, 1, D_MODEL, 2 * D_FF), pick, pipeline_mode=pl.Buffered(1)),
            pl.BlockSpec((1, 1, D_FF, D_MODEL), pick, pipeline_mode=pl.Buffered(1))]


def _ffn_ln_kernel(x_ref, w13_ref, w2_ref, g_ref, b_ref, o_ref):
    x = x_ref[...]
    o_ref[...] = _layer_norm(ALPHA * x + 0.5 * _swiglu(x.astype(BF16), w13_ref, w2_ref), g_ref[...], b_ref[...])


def _ffn_ln(x2d, w13, w2, layer, which, g, b):
    n = x2d.shape[0]
    row = pl.BlockSpec((ROW_TILE, D_MODEL), lambda i: (i, 0))
    return pl.pallas_call(
        _ffn_ln_kernel,
        out_shape=jax.ShapeDtypeStruct((n, D_MODEL), F32),
        grid=(n // ROW_TILE,),
        in_specs=[row, *_ffn_weight_specs(layer, which), _resident((1, D_MODEL)), _resident((1, D_MODEL))],
        out_specs=row,
        compiler_params=_params("parallel"),
        name="ffn_ln",
    )(x2d, w13, w2, g.reshape(1, -1), b.reshape(1, -1))


def _in_proj_kernel(x_ref, wn_ref, wt_ref, ck_ref, cv_ref, ret_ref, ka_ref, kw_ref, qt_ref, vst_ref, vwt_ref,
                    gt_ref):
    xb = x_ref[0].astype(BF16)
    tm = xb.shape[0]
    nat = jnp.dot(xb, wn_ref[...], preferred_element_type=F32)
    ck_ref[0] = nat[:, NAT_CMP:NAT_CMP + KV_WIDTH]
    cv_ref[0] = nat[:, NAT_CMP + KV_WIDTH:NAT_RET]
    ret_ref[0] = nat[:, NAT_RET:NAT_KS]
    ns = ka_ref.shape[3] - Q_BLOCK
    row = pl.program_id(1) * tm + lax.broadcasted_iota(jnp.int32, (tm, ns), 0)
    own = row // SLC_BLOCK == lax.broadcasted_iota(jnp.int32, (tm, ns), 1)
    bias = jnp.where(own, NEG, 0.0).astype(BF16)
    for g in range(NSA_GROUPS):
        ka_ref[0, g, :, 0:Q_BLOCK] = nat[:, NAT_KS + g * Q_BLOCK:NAT_KS + (g + 1) * Q_BLOCK].astype(BF16)
        ka_ref[0, g, :, Q_BLOCK:] = bias
        kw_ref[0, g] = nat[:, NAT_KW + g * Q_BLOCK:NAT_KW + (g + 1) * Q_BLOCK].astype(BF16)
    tr = lax.dot_general(wt_ref[...], xb, (((1,), (1,)), ((), ())), preferred_element_type=F32)
    qt_ref[0] = (tr[TR_Q:TR_VS] * (NSA_HEAD_DIM ** -0.5 * LOG2_E)).astype(BF16)
    vst_ref[0] = tr[TR_VS:TR_VW].astype(BF16)
    vwt_ref[0] = tr[TR_VW:TR_G].astype(BF16)
    gt_ref[0] = tr[TR_G:]


def _split_w_in(w_in):
    c_q = NSA_WIDTH
    c_kv = c_q + 2 * N_BRANCH * KV_WIDTH
    c_g = c_kv + NSA_GROUPS * NSA_REP * N_BRANCH
    kv = w_in[:, c_q:c_kv].reshape(D_MODEL, 2 * N_BRANCH, KV_WIDTH)
    gates = w_in[:, c_kv:c_g].reshape(D_MODEL, NSA_GROUPS, NSA_REP * N_BRANCH)
    gates = jnp.pad(gates, ((0, 0), (0, 0), (0, GATE_ROWS - NSA_REP * N_BRANCH)))
    pad_groups = lambda w: jnp.pad(w.reshape(D_MODEL, NSA_GROUPS, NSA_HEAD_DIM),
                                   ((0, 0), (0, 0), (0, Q_BLOCK - NSA_HEAD_DIM))).reshape(D_MODEL, -1)
    w_nat = jnp.concatenate([kv[:, 0], kv[:, 1], w_in[:, c_g:], pad_groups(kv[:, 2]), pad_groups(kv[:, 4])],
                            axis=1)
    w_tr = jnp.concatenate([w_in[:, :c_q], kv[:, 3], kv[:, 5],
                            gates.reshape(D_MODEL, NSA_GROUPS * GATE_ROWS)], axis=1)
    return w_nat.astype(BF16), w_tr.T.astype(BF16)


def _in_proj(x, w_in):
    bsz, t, _ = x.shape
    g = NSA_GROUPS
    ns = t // SLC_BLOCK
    w_nat, w_trt = _split_w_in(w_in)
    rows = lambda w: pl.BlockSpec((1, ROW_TILE, w), lambda b, i: (b, i, 0))
    cols = lambda h: pl.BlockSpec((1, h, ROW_TILE), lambda b, i: (b, 0, i))
    grouped = lambda w: pl.BlockSpec((1, g, ROW_TILE, w), lambda b, i: (b, 0, i, 0))
    return pl.pallas_call(
        _in_proj_kernel,
        out_shape=(jax.ShapeDtypeStruct((bsz, t, KV_WIDTH), F32),
                   jax.ShapeDtypeStruct((bsz, t, KV_WIDTH), F32),
                   jax.ShapeDtypeStruct((bsz, t, NAT_KS - NAT_RET), F32),
                   jax.ShapeDtypeStruct((bsz, g, t, Q_BLOCK + ns), BF16),
                   jax.ShapeDtypeStruct((bsz, g, t, Q_BLOCK), BF16),
                   jax.ShapeDtypeStruct((bsz, TR_VS - TR_Q, t), BF16),
                   jax.ShapeDtypeStruct((bsz, TR_VW - TR_VS, t), BF16),
                   jax.ShapeDtypeStruct((bsz, TR_G - TR_VW, t), BF16),
                   jax.ShapeDtypeStruct((bsz, TR_ROWS - TR_G, t), F32)),
        grid=(bsz, t // ROW_TILE),
        in_specs=[rows(D_MODEL), _resident((D_MODEL, NAT_COLS)), _resident((TR_ROWS, D_MODEL))],
        out_specs=(rows(KV_WIDTH), rows(KV_WIDTH), rows(NAT_KS - NAT_RET), grouped(Q_BLOCK + ns), grouped(Q_BLOCK),
                   cols(TR_VS - TR_Q), cols(TR_VW - TR_VS), cols(TR_G - TR_VW), cols(TR_ROWS - TR_G)),
        compiler_params=_params("parallel", "parallel"),
        name="in_proj",
    )(x, w_nat, w_trt)


def _gelu_tanh(v):
    return 0.5 * v * (1.0 + jnp.tanh(math.sqrt(2.0 / math.pi) * (v + 0.044715 * (v * v * v))))


def _compress_hidden(h, pos_ref, w1_ref):
    top = jnp.dot((h + pos_ref[0:1, :]).astype(BF16), w1_ref[0], preferred_element_type=F32)
    bot = jnp.dot((h + pos_ref[1:2, :]).astype(BF16), w1_ref[1], preferred_element_type=F32)
    nc = h.shape[0]
    return _gelu_tanh(top + pltpu.roll(bot, nc - 1, 0)).astype(BF16)


def _compress_kernel(rk_ref, rv_ref, pk_ref, pv_ref, w1k_ref, w1v_ref, w2k_ref, w2vt_ref, kc_ref, vct_ref):
    dh = NSA_HEAD_DIM
    nc = kc_ref.shape[2]

    def half_blocks(raw_ref, g):
        rows = [raw_ref[0, pl.ds(l, nc, stride=CMP_STRIDE), :] for l in range(CMP_STRIDE)]
        return jnp.concatenate([r[:, g * dh:(g + 1) * dh] for r in rows], axis=1)

    row = lax.broadcasted_iota(jnp.int32, (nc, dh), 0)
    col = lax.broadcasted_iota(jnp.int32, (dh, nc), 1)
    for g in range(NSA_GROUPS):
        gk = _compress_hidden(half_blocks(rk_ref, g), pk_ref, w1k_ref)
        kc = jnp.dot(gk, w2k_ref[...], preferred_element_type=F32)
        kc_ref[0, g] = jnp.where(row < nc - 1, kc, 0.0).astype(BF16)
        gv = _compress_hidden(half_blocks(rv_ref, g), pv_ref, w1v_ref)
        vct = lax.dot_general(w2vt_ref[...], gv, (((1,), (1,)), ((), ())),
                              preferred_element_type=F32)
        vct_ref[0, g, 0:dh, :] = jnp.where(col < nc - 1, vct, 0.0).astype(BF16)
        vct_ref[0, g, dh:, :] = jnp.ones((SUM_ROWS, nc), BF16)


def _compress(raw_k, raw_v, cmp_pos, cmp_w1, cmp_w2):
    bsz, t, width = raw_k.shape
    g, dh = NSA_GROUPS, NSA_HEAD_DIM
    nc = t // CMP_STRIDE
    half = CMP_BLOCK // 2
    hw = half * dh
    pos = cmp_pos.reshape(2, 2, hw)
    w1 = cmp_w1.reshape(2, 2, hw, CMP_HIDDEN).astype(BF16)
    return pl.pallas_call(
        _compress_kernel,
        out_shape=(jax.ShapeDtypeStruct((bsz, g, nc, dh), BF16),
                   jax.ShapeDtypeStruct((bsz, g, dh + SUM_ROWS, nc), BF16)),
        grid=(bsz,),
        in_specs=[pl.BlockSpec((1, t, width), lambda b: (b, 0, 0)), pl.BlockSpec((1, t, width), lambda b: (b, 0, 0)),
                  _resident((2, hw)), _resident((2, hw)),
                  _resident((2, hw, CMP_HIDDEN)), _resident((2, hw, CMP_HIDDEN)),
                  _resident((CMP_HIDDEN, dh)), _resident((dh, CMP_HIDDEN))],
        out_specs=(pl.BlockSpec((1, g, nc, dh), lambda b: (b, 0, 0, 0)),
                   pl.BlockSpec((1, g, dh + SUM_ROWS, nc), lambda b: (b, 0, 0, 0))),
        compiler_params=_params("parallel"),
        name="compress",
    )(raw_k, raw_v, pos[0], pos[1], w1[0], w1[1], cmp_w2[0].astype(BF16), cmp_w2[1].T.astype(BF16))


def _q_lanes(qt_ref, j=0):
    qt = qt_ref[0, :, j * Q_BLOCK:(j + 1) * Q_BLOCK]
    return jnp.concatenate([qt[r * NSA_HEAD_DIM:(r + 1) * NSA_HEAD_DIM, :] for r in range(NSA_REP)], axis=1)


def _lane_token(shape, q0):
    return q0 + (lax.broadcasted_iota(jnp.int32, shape, 1) & (Q_BLOCK - 1))


def _nsa_select_kernel(qt_ref, kc_ref, vct_ref, ovl_ref, bias_ref, oct_ref, nsel_ref, imp_ref):
    step = pl.program_id(2)
    dh = NSA_HEAD_DIM
    nc = kc_ref.shape[2]
    kc = kc_ref[0, 0]
    lhs = jnp.concatenate([ovl_ref[...], vct_ref[0, 0]], axis=0)
    n_blk = ovl_ref.shape[0]

    def scores(j):
        qb = step * SEL_BLOCKS + j
        s = jnp.dot(kc, _q_lanes(qt_ref, j), preferred_element_type=F32)
        bias = bias_ref[pl.ds(pl.multiple_of(nc - qb * (Q_BLOCK // CMP_STRIDE), 8), nc), :]
        return s + jnp.concatenate([bias] * NSA_REP, axis=1)

    def select(pair):
        lanes = slice(pair * 2 * Q_BLOCK, (pair + 1) * 2 * Q_BLOCK)
        imp = imp_ref[:, lanes]
        blk = lax.broadcasted_iota(jnp.int32, imp.shape, 0)
        t = ((step * SEL_BLOCKS + 2 * pair) * Q_BLOCK
             + lax.broadcasted_iota(jnp.int32, imp.shape, 1))
        cur = t // SLC_BLOCK
        valid = blk <= cur
        forced = (blk == 0) | (blk >= cur - 1)
        val0 = jnp.where(valid & jnp.logical_not(forced), imp, -jnp.inf)
        blk_f = blk.astype(F32)
        val = val0
        for _ in range(max(min(SLC_TOPK, n_blk) - 3, 0)):
            top = jnp.max(val, axis=0, keepdims=True)
            first = jnp.min(jnp.where(val == top, blk_f, float(n_blk)), axis=0, keepdims=True)
            val = jnp.where(blk_f == first, -jnp.inf, val)
            yield
        chosen = valid & (forced | (val != val0))
        nsel_ref[0, 0, :, lanes] = jnp.where(chosen & (blk < 2 * (t // Q_BLOCK)), 0.0, 1.0).astype(BF16)

    running = []

    def advance(n):
        for _ in range(n):
            if running and next(running[0], True):
                running.pop(0)

    s_next = scores(0)
    for j in range(SEL_BLOCKS):
        qb = step * SEL_BLOCKS + j
        s = s_next
        if j + 1 < SEL_BLOCKS:
            s_next = scores(j + 1)
        advance(3)
        e = jnp.exp2(s - jnp.max(s, axis=0, keepdims=True)).astype(BF16)
        advance(3)
        r = jnp.dot(lhs, e, preferred_element_type=F32)
        tok = _lane_token((1, LANES_Q), qb * Q_BLOCK)
        scale = jnp.where(tok >= CMP_BLOCK - 1, 1.0 / r[n_blk + dh:n_blk + dh + 1], 0.0)
        o = r[n_blk:n_blk + dh] * scale
        w = r[0:n_blk] * scale
        imp = w[:, 0:Q_BLOCK]
        for h in range(NSA_REP):
            oct_ref[0, h, :, j * Q_BLOCK:(j + 1) * Q_BLOCK] = o[:, h * Q_BLOCK:(h + 1) * Q_BLOCK]
            if h:
                imp = imp + w[:, h * Q_BLOCK:(h + 1) * Q_BLOCK]
        imp_ref[:, j * Q_BLOCK:(j + 1) * Q_BLOCK] = imp
        advance(3)
        if j % 2 == 1:
            running.append(select(j // 2))
    while running:
        advance(1)


def _cmp_bias_table(t):
    nc = t // CMP_STRIDE
    rel = (np.arange(2 * nc)[:, None] - nc) * CMP_STRIDE + CMP_BLOCK - 1
    return jnp.asarray(np.where(rel <= np.arange(Q_BLOCK)[None, :], 0.0, NEG), dtype=F32)


def _overlap_t(t):
    n_cmp = t // CMP_STRIDE
    n_slc = t // SLC_BLOCK
    cs = np.arange(n_cmp) * CMP_STRIDE
    ss = np.arange(n_slc) * SLC_BLOCK
    ov = np.clip(np.minimum(cs[None, :] + CMP_BLOCK, ss[:, None] + SLC_BLOCK)
                 - np.maximum(cs[None, :], ss[:, None]), 0, None) / CMP_BLOCK
    return jnp.asarray(ov, dtype=BF16)


def _nsa_select(qt, kc, vct):
    bsz, _, t = qt.shape
    g, nc, dh = kc.shape[1:]
    ns = t // SLC_BLOCK
    sq = SEL_BLOCKS * Q_BLOCK
    assert ns % 16 == 0 and t % sq == 0 and SEL_BLOCKS % 2 == 0
    return pl.pallas_call(
        _nsa_select_kernel,
        out_shape=(jax.ShapeDtypeStruct((bsz, g * NSA_REP, dh, t), F32),
                   jax.ShapeDtypeStruct((bsz, g, ns, t), BF16)),
        grid=(bsz, g, t // sq),
        in_specs=[pl.BlockSpec((1, GROUP_Q, sq), lambda b, j, i: (b, j, i)),
                  pl.BlockSpec((1, 1, nc, dh), lambda b, j, i: (b, j, 0, 0)),
                  pl.BlockSpec((1, 1, dh + SUM_ROWS, nc), lambda b, j, i: (b, j, 0, 0)),
                  _resident((ns, nc)), _resident((2 * nc, Q_BLOCK))],
        out_specs=(pl.BlockSpec((1, NSA_REP, dh, sq), lambda b, j, i: (b, j, 0, i)),
                   pl.BlockSpec((1, 1, ns, sq), lambda b, j, i: (b, j, 0, i))),
        scratch_shapes=[pltpu.VMEM((ns, sq), F32)],
        compiler_params=_params("parallel", "parallel", "parallel"),
        name="nsa_select",
    )(qt, kc, vct, _overlap_t(t), _cmp_bias_table(t))


def _nsa_main_kernel(qt_ref, nsel_ref, ka_ref, vst_ref, kw_ref, vwt_ref, oct_ref, gt_ref, band_ref,
                     o_ref, s0_ref, s1_ref, p0_ref, p1_ref, acc_ref, ow_ref, *, key_tile):
    for j in range(MAIN_BLOCKS):
        _nsa_main_block(j, qt_ref, nsel_ref, ka_ref, vst_ref, kw_ref, vwt_ref, oct_ref, gt_ref, band_ref,
                        o_ref, s0_ref.at[j], s1_ref.at[j], p0_ref.at[j], p1_ref.at[j], acc_ref.at[j],
                        ow_ref.at[j], key_tile)


def _nsa_main_block(j, qt_ref, nsel_ref, ka_ref, vst_ref, kw_ref, vwt_ref, oct_ref, gt_ref, band_ref,
                    o_ref, s0_ref, s1_ref, p0_ref, p1_ref, acc_ref, ow_ref, key_tile):
    tok = slice(j * Q_BLOCK, (j + 1) * Q_BLOCK)
    qb = pl.program_id(2) * MAIN_BLOCKS + j
    q0 = qb * Q_BLOCK
    dh = NSA_HEAD_DIM
    last_tile = ka_ref.shape[2] // key_tile - 1
    qm = _q_lanes(qt_ref, j)
    nsel = nsel_ref[0, 0, :, tok]
    q_pad = jnp.concatenate([qm, jnp.zeros((Q_BLOCK - dh, LANES_Q), BF16)], axis=0)
    q_aug = jnp.concatenate([q_pad, jnp.concatenate([nsel] * NSA_REP, axis=1)], axis=0)

    half = LANES_Q // 2
    chunk = key_tile // KEY_CHUNKS
    q_half = [q_aug[:, h * half:(h + 1) * half] for h in range(2)]
    pieces = [(h, c) for c in range(KEY_CHUNKS) for h in range(2)]

    def step(s_in, p_out, col_max, m, l, kt_v, p_in, a_v, kt_n, s_out):
        m_new = jnp.maximum(m, col_max)
        a = jnp.exp2(m - m_new)
        kv0 = pl.multiple_of(kt_v * key_tile, key_tile)
        kn0 = pl.multiple_of(kt_n * key_tile, key_tile)
        sums = [None, None]
        new_max = [None, None]
        acc = [acc_ref[:, h * half:(h + 1) * half] * a_v[:, h * half:(h + 1) * half] for h in range(2)]
        for h, c in pieces:
            lanes = slice(h * half, (h + 1) * half)
            rows = slice(c * chunk, (c + 1) * chunk)
            p = jnp.exp2(s_in[rows, lanes] - m_new[:, lanes])
            p_out[rows, lanes] = p.astype(BF16)
            ps = jnp.sum(p, axis=0, keepdims=True)
            sums[h] = ps if sums[h] is None else sums[h] + ps
            acc[h] = acc[h] + jnp.dot(vst_ref[0, :, pl.ds(kv0 + c * chunk, chunk)], p_in[rows, lanes],
                                      preferred_element_type=F32)
            s = jnp.dot(ka_ref[0, 0, pl.ds(kn0 + c * chunk, chunk), :], q_half[h],
                        preferred_element_type=F32)
            s_out[rows, lanes] = s
            cm = jnp.max(s, axis=0, keepdims=True)
            new_max[h] = cm if new_max[h] is None else jnp.maximum(new_max[h], cm)
        for h in range(2):
            acc_ref[:, h * half:(h + 1) * half] = acc[h]
        return (m_new, l * a + jnp.concatenate(sums, axis=1), a, jnp.concatenate(new_max, axis=1))

    def pair(j, carry):
        m, l, a_prev, cm0 = carry
        t0 = 2 * j
        m, l, a0, cm1 = step(s0_ref, p0_ref, cm0, m, l,
                             jnp.maximum(t0 - 1, 0), p1_ref, a_prev, t0 + 1, s1_ref)
        m, l, a1, cm0 = step(s1_ref, p1_ref, cm1, m, l,
                             t0, p0_ref, a0, jnp.minimum(t0 + 2, last_tile), s0_ref)
        return m, l, a1, cm0

    def scores(kt, s_ref):
        k0 = pl.multiple_of(kt * key_tile, key_tile)
        s = jnp.dot(ka_ref[0, 0, pl.ds(k0, key_tile), :], q_aug, preferred_element_type=F32)
        s_ref[...] = s
        return jnp.max(s, axis=0, keepdims=True)

    def values(kt, p_ref, a):
        k0 = pl.multiple_of(kt * key_tile, key_tile)
        acc_ref[...] = acc_ref[...] * a + jnp.dot(vst_ref[0, :, pl.ds(k0, key_tile)], p_ref[...],
                                                  preferred_element_type=F32)

    p1_ref[...] = jnp.zeros_like(p1_ref)
    cm_first = scores(0, s0_ref)

    d0 = pl.multiple_of(q0, Q_BLOCK)
    s = jnp.dot(ka_ref[0, 0, pl.ds(d0, Q_BLOCK), 0:Q_BLOCK], q_pad, preferred_element_type=F32)
    tq = lax.broadcasted_iota(jnp.int32, s.shape, 1) & (Q_BLOCK - 1)
    kl = lax.broadcasted_iota(jnp.int32, s.shape, 0)
    s = jnp.where(kl <= tq, s, NEG)
    m_own = jnp.max(s, axis=0, keepdims=True)
    p = jnp.exp2(s - m_own)
    acc_ref[...] = jnp.dot(vst_ref[0, :, pl.ds(d0, Q_BLOCK)], p.astype(BF16),
                           preferred_element_type=F32)

    wlen = WINDOW + Q_BLOCK
    w0 = pl.multiple_of(jnp.maximum(q0 - WINDOW, 0), Q_BLOCK)
    band = band_ref[pl.ds(pl.multiple_of(WINDOW - (q0 - w0), Q_BLOCK), wlen), :]
    sw = (jnp.dot(kw_ref[0, 0, pl.ds(w0, wlen), :], q_pad, preferred_element_type=F32)
          + jnp.concatenate([band] * NSA_REP, axis=1))
    e = jnp.exp2(sw - jnp.max(sw, axis=0, keepdims=True))
    ow_ref[...] = (jnp.dot(vwt_ref[0, :, pl.ds(w0, wlen)], e.astype(BF16), preferred_element_type=F32)
                   / jnp.sum(e, axis=0, keepdims=True))

    n_pairs = ((q0 + key_tile - 1) // key_tile + 1) // 2
    init = (m_own, jnp.sum(p, axis=0, keepdims=True), jnp.ones((1, LANES_Q), F32), cm_first)
    _, l, a_last, _ = lax.fori_loop(0, n_pairs, pair, init)
    values(jnp.maximum(2 * n_pairs - 1, 0), p1_ref, a_last)
    o_s = acc_ref[...] / l
    o_w = ow_ref[...]

    gates = _sigmoid(gt_ref[0, :, tok])
    outs = []
    for r in range(NSA_REP):
        sl = slice(r * Q_BLOCK, (r + 1) * Q_BLOCK)
        g_c, g_s, g_w = (gates[r * N_BRANCH + i:r * N_BRANCH + i + 1, :] for i in range(N_BRANCH))
        outs.append(g_c * oct_ref[0, r, :, tok] + g_s * o_s[:, sl] + g_w * o_w[:, sl])
    o_ref[0, tok, :] = jnp.concatenate(outs, axis=0).T.astype(BF16)


def _nsa_main(qt, nsel, k_aug, kw, vs_t, vw_t, oct_, gt, key_tile):
    bsz, _, t = qt.shape
    g = k_aug.shape[1]
    dh = NSA_HEAD_DIM
    ns = t // SLC_BLOCK
    mq = MAIN_BLOCKS * Q_BLOCK
    aw = k_aug.shape[-1]
    assert (t // key_tile) % 2 == 0 and key_tile % Q_BLOCK == 0 and t >= WINDOW + Q_BLOCK and t % mq == 0
    rel = np.arange(2 * WINDOW + Q_BLOCK)[:, None] - WINDOW - np.arange(Q_BLOCK)[None, :]
    band = jnp.asarray(np.where((rel <= 0) & (rel > -WINDOW), 0.0, NEG), dtype=F32)
    return pl.pallas_call(
        functools.partial(_nsa_main_kernel, key_tile=key_tile),
        out_shape=jax.ShapeDtypeStruct((bsz, t, NSA_WIDTH), BF16),
        grid=(bsz, g, t // mq),
        in_specs=[pl.BlockSpec((1, GROUP_Q, mq), lambda b, j, i: (b, j, i)),
                  pl.BlockSpec((1, 1, ns, mq), lambda b, j, i: (b, j, 0, i)),
                  pl.BlockSpec((1, 1, t, aw), lambda b, j, i: (b, j, 0, 0)),
                  pl.BlockSpec((1, dh, t), lambda b, j, i: (b, j, 0)),
                  pl.BlockSpec((1, 1, t, Q_BLOCK), lambda b, j, i: (b, j, 0, 0)),
                  pl.BlockSpec((1, dh, t), lambda b, j, i: (b, j, 0)),
                  pl.BlockSpec((1, NSA_REP, dh, mq), lambda b, j, i: (b, j, 0, i)),
                  pl.BlockSpec((1, GATE_ROWS, mq), lambda b, j, i: (b, j, i)),
                  _resident(band.shape)],
        out_specs=pl.BlockSpec((1, mq, GROUP_Q), lambda b, j, i: (b, i, j)),
        scratch_shapes=[pltpu.VMEM((MAIN_BLOCKS, key_tile, LANES_Q), F32),
                        pltpu.VMEM((MAIN_BLOCKS, key_tile, LANES_Q), F32),
                        pltpu.VMEM((MAIN_BLOCKS, key_tile, LANES_Q), BF16),
                        pltpu.VMEM((MAIN_BLOCKS, key_tile, LANES_Q), BF16),
                        pltpu.VMEM((MAIN_BLOCKS, dh, LANES_Q), F32), pltpu.VMEM((MAIN_BLOCKS, dh, LANES_Q), F32)],
        compiler_params=_params("parallel", "parallel", "arbitrary"),
        name="nsa_main",
    )(qt, nsel, k_aug, vs_t, kw, vw_t, oct_, gt, band)


def _retention_kernel(q_ref, k_ref, v_ref, gate_ref, cos_ref, sin_ref, gng_ref, gnb_ref,
                      o_ref, state_ref, decay_ref, xi_ref, zeta_ref):
    c = RET_CHUNK
    d = RET_HEAD_DIM

    @pl.when(pl.program_id(1) == 0)
    def _():
        state_ref[...] = jnp.zeros_like(state_ref)
        i_row = lax.broadcasted_iota(jnp.int32, (c, c), 0)
        rel = (i_row - lax.broadcasted_iota(jnp.int32, (c, c), 1)).astype(F32)
        pos = i_row.astype(F32)
        for h in range(RET_HEADS):
            log_g = math.log(1.0 - 2.0 ** (-5.0 - h))
            decay_ref[h] = jnp.where(rel >= 0, jnp.exp(jnp.maximum(rel, 0.0) * log_g), 0.0)
            xi_ref[h] = jnp.exp((pos + 1.0) * log_g)
            zeta_ref[h] = jnp.exp((c - 1.0 - pos) * log_g) * (d ** -0.5)

    for sub in range(q_ref.shape[1] // c):
        rows = slice(sub * c, (sub + 1) * c)
        cos = cos_ref[rows, :]
        sin = sin_ref[rows, :]
        for h in range(RET_HEADS):
            log_g = math.log(1.0 - 2.0 ** (-5.0 - h))
            sl = slice(h * d, (h + 1) * d)
            q = q_ref[0, rows, sl]
            k = k_ref[0, rows, sl]
            q = q * cos + pltpu.roll(q, d // 2, 1) * sin
            k = k * cos + pltpu.roll(k, d // 2, 1) * sin
            vb = v_ref[0, rows, sl].astype(BF16)
            inner = lax.dot_general(q.astype(BF16), (k * (d ** -0.5)).astype(BF16), (((1,), (1,)), ((), ())),
                                    preferred_element_type=F32) * decay_ref[h]
            o = jnp.dot(inner.astype(BF16), vb, preferred_element_type=F32)
            state = state_ref[h]
            o = o + jnp.dot((q * xi_ref[h]).astype(BF16), state.astype(BF16), preferred_element_type=F32)
            kv = jnp.dot((k * zeta_ref[h]).T.astype(BF16), vb, preferred_element_type=F32)
            state_ref[h] = math.exp(c * log_g) * state + kv
            mu = jnp.mean(o, axis=-1, keepdims=True)
            oc = o - mu
            var = jnp.mean(oc * oc, axis=-1, keepdims=True)
            y = oc * lax.rsqrt(var + LN_EPS) * gng_ref[:, sl] + gnb_ref[:, sl]
            gate = gate_ref[0, rows, sl]
            o_ref[0, rows, sl] = (gate * _sigmoid(gate) * y).astype(BF16)


def _rotary_tables(t):
    d = RET_HEAD_DIM
    ang = np.arange(t)[:, None] * ROPE_BASE ** (-np.arange(0, d, 2) / d)[None, :]
    cos, sin = np.cos(ang), np.sin(ang)
    return (jnp.asarray(np.concatenate([cos, cos], axis=1), dtype=F32),
            jnp.asarray(np.concatenate([-sin, sin], axis=1), dtype=F32))


def _retention(ret, gn_g, gn_b):
    bsz, t, _ = ret.shape
    c = RET_STEP
    cos2, sin2 = _rotary_tables(t)
    part = lambda j: pl.BlockSpec((1, c, RET_WIDTH), lambda b, n: (b, n, j))
    tab = pl.BlockSpec((c, RET_HEAD_DIM), lambda b, n: (n, 0))
    head_mat = pltpu.VMEM((RET_HEADS, RET_CHUNK, RET_HEAD_DIM), F32)
    return pl.pallas_call(
        _retention_kernel,
        out_shape=jax.ShapeDtypeStruct((bsz, t, RET_WIDTH), BF16),
        grid=(bsz, t // c),
        in_specs=[part(0), part(1), part(2), part(3), tab, tab,
                  _resident((1, RET_WIDTH)), _resident((1, RET_WIDTH))],
        out_specs=pl.BlockSpec((1, c, RET_WIDTH), lambda b, n: (b, n, 0)),
        scratch_shapes=[pltpu.VMEM((RET_HEADS, RET_HEAD_DIM, RET_HEAD_DIM), F32), head_mat, head_mat, head_mat],
        compiler_params=_params("parallel", "arbitrary"),
        name="retention",
    )(ret, ret, ret, ret, cos2, sin2, gn_g.reshape(1, -1), gn_b.reshape(1, -1))


def _tail_kernel(x_ref, on_ref, or_ref, p_ref, wn_ref, wr_ref, w13_ref, w2_ref, wp_ref, wg_ref,
                 g_ref, b_ref, o_ref):
    mix = (jnp.dot(on_ref[...], wn_ref[...], preferred_element_type=F32)
           + jnp.dot(or_ref[...], wr_ref[...], preferred_element_type=F32))
    x = _layer_norm(ALPHA * x_ref[...] + mix, g_ref[0:1, :], b_ref[0:1, :])
    x = _layer_norm(ALPHA * x + 0.5 * _swiglu(x.astype(BF16), w13_ref, w2_ref), g_ref[1:2, :], b_ref[1:2, :])
    e = (jnp.dot(p_ref[...].astype(BF16), wp_ref[...], preferred_element_type=F32)
         * _sigmoid(jnp.dot(x.astype(BF16), wg_ref[...], preferred_element_type=F32)))
    o_ref[...] = _layer_norm(ALPHA * x + e, g_ref[2:3, :], b_ref[2:3, :])


def _tail(x2d, o_nsa, o_ret, p2d, w_out, w13, w2, layer, w_ple, w_gate, g, b):
    n = x2d.shape[0]
    row = lambda w: pl.BlockSpec((ROW_TILE, w), lambda i: (i, 0))
    return pl.pallas_call(
        _tail_kernel,
        out_shape=jax.ShapeDtypeStruct((n, D_MODEL), F32),
        grid=(n // ROW_TILE,),
        in_specs=[row(D_MODEL), row(NSA_WIDTH), row(RET_WIDTH), row(PLE_DIM),
                  _resident((NSA_WIDTH, D_MODEL)), _resident((RET_WIDTH, D_MODEL)),
                  *_ffn_weight_specs(layer, 1),
                  _resident((PLE_DIM, D_MODEL)), _resident((D_MODEL, D_MODEL)),
                  _resident((3, D_MODEL)), _resident((3, D_MODEL))],
        out_specs=row(D_MODEL),
        compiler_params=_params("parallel"),
        name="tail",
    )(x2d, o_nsa, o_ret, p2d, w_out[:NSA_WIDTH].astype(BF16), w_out[NSA_WIDTH:].astype(BF16),
      w13, w2, w_ple.astype(BF16), w_gate.astype(BF16), g, b)


def _token_mixer(x, w_in, cmp_pos, cmp_w1, cmp_w2, gn_g, gn_b):
    t = x.shape[1]
    raw_k, raw_v, ret, k_aug, kw, qt, vs_t, vw_t, gt = _in_proj(x, w_in)
    kc, vct = _compress(raw_k, raw_v, cmp_pos, cmp_w1, cmp_w2)
    oct_, nsel = _nsa_select(qt, kc, vct)
    o_nsa = _nsa_main(qt, nsel, k_aug, kw, vs_t, vw_t, oct_, gt, min(512, t))
    o_ret = _retention(ret, gn_g, gn_b)
    return o_nsa, o_ret


def kernel(x, p, ffn_w13, ffn_w2, w_in, cmp_pos, cmp_w1, cmp_w2, ret_gn_g, ret_gn_b,
           w_out, w_ple, w_ple_gate, ln_g, ln_b):
    bsz, t, d = x.shape
    n = bsz * t
    w13, w2 = ffn_w13.astype(BF16), ffn_w2.astype(BF16)
    for i in range(ffn_w13.shape[0]):
        h = _ffn_ln(x.reshape(n, d), w13, w2, i, 0, ln_g[i, 0], ln_b[i, 0])
        o_nsa, o_ret = _token_mixer(h.reshape(bsz, t, d), w_in[i], cmp_pos[i], cmp_w1[i], cmp_w2[i],
                                    ret_gn_g[i], ret_gn_b[i])
        h = _tail(h, o_nsa.reshape(n, -1), o_ret.reshape(n, -1), p[i].reshape(n, -1), w_out[i],
                  w13, w2, i, w_ple[i], w_ple_gate[i], ln_g[i, 1:], ln_b[i, 1:])
        x = h.reshape(bsz, t, d)
    return x
```

```python
import functools
import math

import jax
import jax.numpy as jnp
import numpy as np
from jax import lax
from jax.experimental import pallas as pl
from jax.experimental.pallas import tpu as pltpu

F32 = jnp.float32
BF16 = jnp.bfloat16

D_MODEL = 1024
D_FF = 2816
NSA_GROUPS = 2
NSA_REP = 4
NSA_HEAD_DIM = 64
N_BRANCH = 3
CMP_BLOCK = 32
CMP_STRIDE = 16
CMP_HIDDEN = 256
SLC_BLOCK = 64
SLC_TOPK = 16
WINDOW = 512
Q_BLOCK = 128
RET_HEADS = 4
RET_HEAD_DIM = 128
RET_CHUNK = 128
RET_STEP = 256
ROPE_BASE = 10000.0
PLE_DIM = 256
DEPTH = 1
ALPHA = (2.0 * DEPTH) ** 0.25
LN_EPS = 1e-5
NEG = -1e30
LOG2_E = math.log2(math.e)

NSA_WIDTH = NSA_GROUPS * NSA_REP * NSA_HEAD_DIM
RET_WIDTH = RET_HEADS * RET_HEAD_DIM
GROUP_Q = NSA_REP * NSA_HEAD_DIM
LANES_Q = NSA_REP * Q_BLOCK
GATE_ROWS = 16
KV_WIDTH = NSA_GROUPS * NSA_HEAD_DIM
TR_Q, TR_VS, TR_VW, TR_G = 0, 512, 640, 768
TR_ROWS = TR_G + NSA_GROUPS * GATE_ROWS
NAT_CMP, NAT_RET = 0, 2 * KV_WIDTH
NAT_KS = NAT_RET + 4 * RET_WIDTH
NAT_KW = NAT_KS + NSA_GROUPS * Q_BLOCK
NAT_COLS = NAT_KW + NSA_GROUPS * Q_BLOCK

KEY_CHUNKS = 1
SUM_ROWS = 16
SEL_BLOCKS = 8
MAIN_BLOCKS = 1
VMEM_LIMIT = 56 * 1024 * 1024
ROW_TILE = 512


def _layer_norm(z, g, b):
    mu = jnp.mean(z, axis=-1, keepdims=True)
    zc = z - mu
    var = jnp.mean(zc * zc, axis=-1, keepdims=True)
    return zc * lax.rsqrt(var + LN_EPS) * g + b


def _sigmoid(v):
    return 1.0 / (1.0 + jnp.exp(-v))


def _params(*sem):
    return pltpu.CompilerParams(dimension_semantics=sem, vmem_limit_bytes=VMEM_LIMIT)


def _resident(shape):
    nd = len(shape)
    return pl.BlockSpec(shape, lambda *_: (0,) * nd, pipeline_mode=pl.Buffered(1))


def _swiglu(xb, w13_ref, w2_ref):
    a = jnp.dot(xb, w13_ref[0, 0, :, :D_FF], preferred_element_type=F32)
    u = jnp.dot(xb, w13_ref[0, 0, :, D_FF:], preferred_element_type=F32)
    h = (a * _sigmoid(a) * u).astype(BF16)
    return jnp.dot(h, w2_ref[0, 0], preferred_element_type=F32)


def _ffn_weight_specs(layer, which):
    pick = lambda *_: (layer, which, 0, 0)
    return [pl.BlockSpec((1, 1, D_MODEL, 2 * D_FF), pick, pipeline_mode=pl.Buffered(1)),
            pl.BlockSpec((1, 1, D_FF, D_MODEL), pick, pipeline_mode=pl.Buffered(1))]


def _ffn_ln_kernel(x_ref, w13_ref, w2_ref, g_ref, b_ref, o_ref):
    x = x_ref[...]
    o_ref[...] = _layer_norm(ALPHA * x + 0.5 * _swiglu(x.astype(BF16), w13_ref, w2_ref), g_ref[...], b_ref[...])


def _ffn_ln(x2d, w13, w2, layer, which, g, b):
    n = x2d.shape[0]
    row = pl.BlockSpec((ROW_TILE, D_MODEL), lambda i: (i, 0))
    return pl.pallas_call(
        _ffn_ln_kernel,
        out_shape=jax.ShapeDtypeStruct((n, D_MODEL), F32),
        grid=(n // ROW_TILE,),
        in_specs=[row, *_ffn_weight_specs(layer, which), _resident((1, D_MODEL)), _resident((1, D_MODEL))],
        out_specs=row,
        compiler_params=_params("parallel"),
        name="ffn_ln",
    )(x2d, w13, w2, g.reshape(1, -1), b.reshape(1, -1))


def _in_proj_kernel(x_ref, wn_ref, wt_ref, ck_ref, cv_ref, ret_ref, ka_ref, kw_ref, qt_ref, vst_ref, vwt_ref,
                    gt_ref):
    xb = x_ref[0].astype(BF16)
    tm = xb.shape[0]
    nat = jnp.dot(xb, wn_ref[...], preferred_element_type=F32)
    ck_ref[0] = nat[:, NAT_CMP:NAT_CMP + KV_WIDTH]
    cv_ref[0] = nat[:, NAT_CMP + KV_WIDTH:NAT_RET]
    ret_ref[0] = nat[:, NAT_RET:NAT_KS]
    ns = ka_ref.shape[3] - Q_BLOCK
    row = pl.program_id(1) * tm + lax.broadcasted_iota(jnp.int32, (tm, ns), 0)
    own = row // SLC_BLOCK == lax.broadcasted_iota(jnp.int32, (tm, ns), 1)
    bias = jnp.where(own, NEG, 0.0).astype(BF16)
    for g in range(NSA_GROUPS):
        ka_ref[0, g, :, 0:Q_BLOCK] = nat[:, NAT_KS + g * Q_BLOCK:NAT_KS + (g + 1) * Q_BLOCK].astype(BF16)
        ka_ref[0, g, :, Q_BLOCK:] = bias
        kw_ref[0, g] = nat[:, NAT_KW + g * Q_BLOCK:NAT_KW + (g + 1) * Q_BLOCK].astype(BF16)
    tr = lax.dot_general(wt_ref[...], xb, (((1,), (1,)), ((), ())), preferred_element_type=F32)
    qt_ref[0] = (tr[TR_Q:TR_VS] * (NSA_HEAD_DIM ** -0.5 * LOG2_E)).astype(BF16)
    vst_ref[0] = tr[TR_VS:TR_VW].astype(BF16)
    vwt_ref[0] = tr[TR_VW:TR_G].astype(BF16)
    gt_ref[0] = tr[TR_G:]


def _split_w_in(w_in):
    c_q = NSA_WIDTH
    c_kv = c_q + 2 * N_BRANCH * KV_WIDTH
    c_g = c_kv + NSA_GROUPS * NSA_REP * N_BRANCH
    kv = w_in[:, c_q:c_kv].reshape(D_MODEL, 2 * N_BRANCH, KV_WIDTH)
    gates = w_in[:, c_kv:c_g].reshape(D_MODEL, NSA_GROUPS, NSA_REP * N_BRANCH)
    gates = jnp.pad(gates, ((0, 0), (0, 0), (0, GATE_ROWS - NSA_REP * N_BRANCH)))
    pad_groups = lambda w: jnp.pad(w.reshape(D_MODEL, NSA_GROUPS, NSA_HEAD_DIM),
                                   ((0, 0), (0, 0), (0, Q_BLOCK - NSA_HEAD_DIM))).reshape(D_MODEL, -1)
    w_nat = jnp.concatenate([kv[:, 0], kv[:, 1], w_in[:, c_g:], pad_groups(kv[:, 2]), pad_groups(kv[:, 4])],
                            axis=1)
    w_tr = jnp.concatenate([w_in[:, :c_q], kv[:, 3], kv[:, 5],
                            gates.reshape(D_MODEL, NSA_GROUPS * GATE_ROWS)], axis=1)
    return w_nat.astype(BF16), w_tr.T.astype(BF16)


def _in_proj(x, w_in):
    bsz, t, _ = x.shape
    g = NSA_GROUPS
    ns = t // SLC_BLOCK
    w_nat, w_trt = _split_w_in(w_in)
    rows = lambda w: pl.BlockSpec((1, ROW_TILE, w), lambda b, i: (b, i, 0))
    cols = lambda h: pl.BlockSpec((1, h, ROW_TILE), lambda b, i: (b, 0, i))
    grouped = lambda w: pl.BlockSpec((1, g, ROW_TILE, w), lambda b, i: (b, 0, i, 0))
    return pl.pallas_call(
        _in_proj_kernel,
        out_shape=(jax.ShapeDtypeStruct((bsz, t, KV_WIDTH), F32),
                   jax.ShapeDtypeStruct((bsz, t, KV_WIDTH), F32),
                   jax.ShapeDtypeStruct((bsz, t, NAT_KS - NAT_RET), F32),
                   jax.ShapeDtypeStruct((bsz, g, t, Q_BLOCK + ns), BF16),
                   jax.ShapeDtypeStruct((bsz, g, t, Q_BLOCK), BF16),
                   jax.ShapeDtypeStruct((bsz, TR_VS - TR_Q, t), BF16),
                   jax.ShapeDtypeStruct((bsz, TR_VW - TR_VS, t), BF16),
                   jax.ShapeDtypeStruct((bsz, TR_G - TR_VW, t), BF16),
                   jax.ShapeDtypeStruct((bsz, TR_ROWS - TR_G, t), F32)),
        grid=(bsz, t // ROW_TILE),
        in_specs=[rows(D_MODEL), _resident((D_MODEL, NAT_COLS)), _resident((TR_ROWS, D_MODEL))],
        out_specs=(rows(KV_WIDTH), rows(KV_WIDTH), rows(NAT_KS - NAT_RET), grouped(Q_BLOCK + ns), grouped(Q_BLOCK),
                   cols(TR_VS - TR_Q), cols(TR_VW - TR_VS), cols(TR_G - TR_VW), cols(TR_ROWS - TR_G)),
        compiler_params=_params("parallel", "parallel"),
        name="in_proj",
    )(x, w_nat, w_trt)


def _gelu_tanh(v):
    return 0.5 * v * (1.0 + jnp.tanh(math.sqrt(2.0 / math.pi) * (v + 0.044715 * (v * v * v))))


def _compress_hidden(h, pos_ref, w1_ref):
    top = jnp.dot((h + pos_ref[0:1, :]).astype(BF16), w1_ref[0], preferred_element_type=F32)
    bot = jnp.dot((h + pos_ref[1:2, :]).astype(BF16), w1_ref[1], preferred_element_type=F32)
    nc = h.shape[0]
    return _gelu_tanh(top + pltpu.roll(bot, nc - 1, 0)).astype(BF16)


def _compress_kernel(rk_ref, rv_ref, pk_ref, pv_ref, w1k_ref, w1v_ref, w2k_ref, w2vt_ref, kc_ref, vct_ref):
    dh = NSA_HEAD_DIM
    nc = kc_ref.shape[2]

    def half_blocks(raw_ref, g):
        rows = [raw_ref[0, pl.ds(l, nc, stride=CMP_STRIDE), :] for l in range(CMP_STRIDE)]
        return jnp.concatenate([r[:, g * dh:(g + 1) * dh] for r in rows], axis=1)

    row = lax.broadcasted_iota(jnp.int32, (nc, dh), 0)
    col = lax.broadcasted_iota(jnp.int32, (dh, nc), 1)
    for g in range(NSA_GROUPS):
        gk = _compress_hidden(half_blocks(rk_ref, g), pk_ref, w1k_ref)
        kc = jnp.dot(gk, w2k_ref[...], preferred_element_type=F32)
        kc_ref[0, g] = jnp.where(row < nc - 1, kc, 0.0).astype(BF16)
        gv = _compress_hidden(half_blocks(rv_ref, g), pv_ref, w1v_ref)
        vct = lax.dot_general(w2vt_ref[...], gv, (((1,), (1,)), ((), ())),
                              preferred_element_type=F32)
        vct_ref[0, g, 0:dh, :] = jnp.where(col < nc - 1, vct, 0.0).astype(BF16)
        vct_ref[0, g, dh:, :] = jnp.ones((SUM_ROWS, nc), BF16)


def _compress(raw_k, raw_v, cmp_pos, cmp_w1, cmp_w2):
    bsz, t, width = raw_k.shape
    g, dh = NSA_GROUPS, NSA_HEAD_DIM
    nc = t // CMP_STRIDE
    half = CMP_BLOCK // 2
    hw = half * dh
    pos = cmp_pos.reshape(2, 2, hw)
    w1 = cmp_w1.reshape(2, 2, hw, CMP_HIDDEN).astype(BF16)
    return pl.pallas_call(
        _compress_kernel,
        out_shape=(jax.ShapeDtypeStruct((bsz, g, nc, dh), BF16),
                   jax.ShapeDtypeStruct((bsz, g, dh + SUM_ROWS, nc), BF16)),
        grid=(bsz,),
        in_specs=[pl.BlockSpec((1, t, width), lambda b: (b, 0, 0)), pl.BlockSpec((1, t, width), lambda b: (b, 0, 0)),
                  _resident((2, hw)), _resident((2, hw)),
                  _resident((2, hw, CMP_HIDDEN)), _resident((2, hw, CMP_HIDDEN)),
                  _resident((CMP_HIDDEN, dh)), _resident((dh, CMP_HIDDEN))],
        out_specs=(pl.BlockSpec((1, g, nc, dh), lambda b: (b, 0, 0, 0)),
                   pl.BlockSpec((1, g, dh + SUM_ROWS, nc), lambda b: (b, 0, 0, 0))),
        compiler_params=_params("parallel"),
        name="compress",
    )(raw_k, raw_v, pos[0], pos[1], w1[0], w1[1], cmp_w2[0].astype(BF16), cmp_w2[1].T.astype(BF16))


def _q_lanes(qt_ref, j=0):
    qt = qt_ref[0, :, j * Q_BLOCK:(j + 1) * Q_BLOCK]
    return jnp.concatenate([qt[r * NSA_HEAD_DIM:(r + 1) * NSA_HEAD_DIM, :] for r in range(NSA_REP)], axis=1)


def _lane_token(shape, q0):
    return q0 + (lax.broadcasted_iota(jnp.int32, shape, 1) & (Q_BLOCK - 1))


def _nsa_select_kernel(qt_ref, kc_ref, vct_ref, ovl_ref, bias_ref, oct_ref, nsel_ref, imp_ref):
    step = pl.program_id(2)
    dh = NSA_HEAD_DIM
    nc = kc_ref.shape[2]
    kc = kc_ref[0, 0]
    lhs = jnp.concatenate([ovl_ref[...], vct_ref[0, 0]], axis=0)
    n_blk = ovl_ref.shape[0]

    def scores(j):
        qb = step * SEL_BLOCKS + j
        s = jnp.dot(kc, _q_lanes(qt_ref, j), preferred_element_type=F32)
        bias = bias_ref[pl.ds(pl.multiple_of(nc - qb * (Q_BLOCK // CMP_STRIDE), 8), nc), :]
        return s + jnp.concatenate([bias] * NSA_REP, axis=1)

    def select(pair):
        lanes = slice(pair * 2 * Q_BLOCK, (pair + 1) * 2 * Q_BLOCK)
        imp = imp_ref[:, lanes]
        blk = lax.broadcasted_iota(jnp.int32, imp.shape, 0)
        t = ((step * SEL_BLOCKS + 2 * pair) * Q_BLOCK
             + lax.broadcasted_iota(jnp.int32, imp.shape, 1))
        cur = t // SLC_BLOCK
        valid = blk <= cur
        forced = (blk == 0) | (blk >= cur - 1)
        val0 = jnp.where(valid & jnp.logical_not(forced), imp, -jnp.inf)
        blk_f = blk.astype(F32)
        val = val0
        for _ in range(max(min(SLC_TOPK, n_blk) - 3, 0)):
            top = jnp.max(val, axis=0, keepdims=True)
            first = jnp.min(jnp.where(val == top, blk_f, float(n_blk)), axis=0, keepdims=True)
            val = jnp.where(blk_f == first, -jnp.inf, val)
            yield
        chosen = valid & (forced | (val != val0))
        nsel_ref[0, 0, :, lanes] = jnp.where(chosen & (blk < 2 * (t // Q_BLOCK)), 0.0, 1.0).astype(BF16)

    running = []

    def advance(n):
        for _ in range(n):
            if running and next(running[0], True):
                running.pop(0)

    s_next = scores(0)
    for j in range(SEL_BLOCKS):
        qb = step * SEL_BLOCKS + j
        s = s_next
        if j + 1 < SEL_BLOCKS:
            s_next = scores(j + 1)
        advance(3)
        e = jnp.exp2(s - jnp.max(s, axis=0, keepdims=True)).astype(BF16)
        advance(3)
        r = jnp.dot(lhs, e, preferred_element_type=F32)
        tok = _lane_token((1, LANES_Q), qb * Q_BLOCK)
        scale = jnp.where(tok >= CMP_BLOCK - 1, 1.0 / r[n_blk + dh:n_blk + dh + 1], 0.0)
        o = r[n_blk:n_blk + dh] * scale
        w = r[0:n_blk] * scale
        imp = w[:, 0:Q_BLOCK]
        for h in range(NSA_REP):
            oct_ref[0, h, :, j * Q_BLOCK:(j + 1) * Q_BLOCK] = o[:, h * Q_BLOCK:(h + 1) * Q_BLOCK]
            if h:
                imp = imp + w[:, h * Q_BLOCK:(h + 1) * Q_BLOCK]
        imp_ref[:, j * Q_BLOCK:(j + 1) * Q_BLOCK] = imp
        advance(3)
        if j % 2 == 1:
            running.append(select(j // 2))
    while running:
        advance(1)


def _cmp_bias_table(t):
    nc = t // CMP_STRIDE
    rel = (np.arange(2 * nc)[:, None] - nc) * CMP_STRIDE + CMP_BLOCK - 1
    return jnp.asarray(np.where(rel <= np.arange(Q_BLOCK)[None, :], 0.0, NEG), dtype=F32)


def _overlap_t(t):
    n_cmp = t // CMP_STRIDE
    n_slc = t // SLC_BLOCK
    cs = np.arange(n_cmp) * CMP_STRIDE
    ss = np.arange(n_slc) * SLC_BLOCK
    ov = np.clip(np.minimum(cs[None, :] + CMP_BLOCK, ss[:, None] + SLC_BLOCK)
                 - np.maximum(cs[None, :], ss[:, None]), 0, None) / CMP_BLOCK
    return jnp.asarray(ov, dtype=BF16)


def _nsa_select(qt, kc, vct):
    bsz, _, t = qt.shape
    g, nc, dh = kc.shape[1:]
    ns = t // SLC_BLOCK
    sq = SEL_BLOCKS * Q_BLOCK
    assert ns % 16 == 0 and t % sq == 0 and SEL_BLOCKS % 2 == 0
    return pl.pallas_call(
        _nsa_select_kernel,
        out_shape=(jax.ShapeDtypeStruct((bsz, g * NSA_REP, dh, t), F32),
                   jax.ShapeDtypeStruct((bsz, g, ns, t), BF16)),
        grid=(bsz, g, t // sq),
        in_specs=[pl.BlockSpec((1, GROUP_Q, sq), lambda b, j, i: (b, j, i)),
                  pl.BlockSpec((1, 1, nc, dh), lambda b, j, i: (b, j, 0, 0)),
                  pl.BlockSpec((1, 1, dh + SUM_ROWS, nc), lambda b, j, i: (b, j, 0, 0)),
                  _resident((ns, nc)), _resident((2 * nc, Q_BLOCK))],
        out_specs=(pl.BlockSpec((1, NSA_REP, dh, sq), lambda b, j, i: (b, j, 0, i)),
                   pl.BlockSpec((1, 1, ns, sq), lambda b, j, i: (b, j, 0, i))),
        scratch_shapes=[pltpu.VMEM((ns, sq), F32)],
        compiler_params=_params("parallel", "parallel", "parallel"),
        name="nsa_select",
    )(qt, kc, vct, _overlap_t(t), _cmp_bias_table(t))


def _nsa_main_kernel(qt_ref, nsel_ref, ka_ref, vst_ref, kw_ref, vwt_ref, oct_ref, gt_ref, band_ref,
                     o_ref, s0_ref, s1_ref, p0_ref, p1_ref, acc_ref, ow_ref, *, key_tile):
    for j in range(MAIN_BLOCKS):
        _nsa_main_block(j, qt_ref, nsel_ref, ka_ref, vst_ref, kw_ref, vwt_ref, oct_ref, gt_ref, band_ref,
                        o_ref, s0_ref.at[j], s1_ref.at[j], p0_ref.at[j], p1_ref.at[j], acc_ref.at[j],
                        ow_ref.at[j], key_tile)


def _nsa_main_block(j, qt_ref, nsel_ref, ka_ref, vst_ref, kw_ref, vwt_ref, oct_ref, gt_ref, band_ref,
                    o_ref, s0_ref, s1_ref, p0_ref, p1_ref, acc_ref, ow_ref, key_tile):
    tok = slice(j * Q_BLOCK, (j + 1) * Q_BLOCK)
    qb = pl.program_id(2) * MAIN_BLOCKS + j
    q0 = qb * Q_BLOCK
    dh = NSA_HEAD_DIM
    last_tile = ka_ref.shape[2] // key_tile - 1
    qm = _q_lanes(qt_ref, j)
    nsel = nsel_ref[0, 0, :, tok]
    q_pad = jnp.concatenate([qm, jnp.zeros((Q_BLOCK - dh, LANES_Q), BF16)], axis=0)
    q_aug = jnp.concatenate([q_pad, jnp.concatenate([nsel] * NSA_REP, axis=1)], axis=0)

    half = LANES_Q // 2
    chunk = key_tile // KEY_CHUNKS
    q_half = [q_aug[:, h * half:(h + 1) * half] for h in range(2)]
    pieces = [(h, c) for c in range(KEY_CHUNKS) for h in range(2)]

    def step(s_in, p_out, col_max, m, l, kt_v, p_in, a_v, kt_n, s_out):
        m_new = jnp.maximum(m, col_max)
        a = jnp.exp2(m - m_new)
        kv0 = pl.multiple_of(kt_v * key_tile, key_tile)
        kn0 = pl.multiple_of(kt_n * key_tile, key_tile)
        sums = [None, None]
        new_max = [None, None]
        acc = [acc_ref[:, h * half:(h + 1) * half] * a_v[:, h * half:(h + 1) * half] for h in range(2)]
        for h, c in pieces:
            lanes = slice(h * half, (h + 1) * half)
            rows = slice(c * chunk, (c + 1) * chunk)
            p = jnp.exp2(s_in[h, rows, :] - m_new[:, lanes])
            p_out[h, rows, :] = p.astype(BF16)
            ps = jnp.sum(p, axis=0, keepdims=True)
            sums[h] = ps if sums[h] is None else sums[h] + ps
            acc[h] = acc[h] + jnp.dot(vst_ref[0, :, pl.ds(kv0 + c * chunk, chunk)], p_in[h, rows, :],
                                      preferred_element_type=F32)
            s = jnp.dot(ka_ref[0, 0, pl.ds(kn0 + c * chunk, chunk), :], q_half[h],
                        preferred_element_type=F32)
            s_out[h, rows, :] = s
            cm = jnp.max(s, axis=0, keepdims=True)
            new_max[h] = cm if new_max[h] is None else jnp.maximum(new_max[h], cm)
        for h in range(2):
            acc_ref[:, h * half:(h + 1) * half] = acc[h]
        return (m_new, l * a + jnp.concatenate(sums, axis=1), a, jnp.concatenate(new_max, axis=1))

    def pair(j, carry):
        m, l, a_prev, cm0 = carry
        t0 = 2 * j
        m, l, a0, cm1 = step(s0_ref, p0_ref, cm0, m, l,
                             jnp.maximum(t0 - 1, 0), p1_ref, a_prev, t0 + 1, s1_ref)
        m, l, a1, cm0 = step(s1_ref, p1_ref, cm1, m, l,
                             t0, p0_ref, a0, jnp.minimum(t0 + 2, last_tile), s0_ref)
        return m, l, a1, cm0

    def scores(kt, s_ref):
        k0 = pl.multiple_of(kt * key_tile, key_tile)
        col_max = []
        for h in range(2):
            s = jnp.dot(ka_ref[0, 0, pl.ds(k0, key_tile), :], q_half[h], preferred_element_type=F32)
            s_ref[h] = s
            col_max.append(jnp.max(s, axis=0, keepdims=True))
        return jnp.concatenate(col_max, axis=1)

    def values(kt, p_ref, a):
        k0 = pl.multiple_of(kt * key_tile, key_tile)
        for h in range(2):
            lanes = slice(h * half, (h + 1) * half)
            acc_ref[:, lanes] = acc_ref[:, lanes] * a[:, lanes] + jnp.dot(
                vst_ref[0, :, pl.ds(k0, key_tile)], p_ref[h], preferred_element_type=F32)

    p1_ref[...] = jnp.zeros_like(p1_ref)
    cm_first = scores(0, s0_ref)

    d0 = pl.multiple_of(q0, Q_BLOCK)
    s = jnp.dot(ka_ref[0, 0, pl.ds(d0, Q_BLOCK), 0:Q_BLOCK], q_pad, preferred_element_type=F32)
    tq = lax.broadcasted_iota(jnp.int32, s.shape, 1) & (Q_BLOCK - 1)
    kl = lax.broadcasted_iota(jnp.int32, s.shape, 0)
    s = jnp.where(kl <= tq, s, NEG)
    m_own = jnp.max(s, axis=0, keepdims=True)
    p = jnp.exp2(s - m_own)
    acc_ref[...] = jnp.dot(vst_ref[0, :, pl.ds(d0, Q_BLOCK)], p.astype(BF16),
                           preferred_element_type=F32)

    wlen = WINDOW + Q_BLOCK
    w0 = pl.multiple_of(jnp.maximum(q0 - WINDOW, 0), Q_BLOCK)
    band = band_ref[pl.ds(pl.multiple_of(WINDOW - (q0 - w0), Q_BLOCK), wlen), :]
    sw = (jnp.dot(kw_ref[0, 0, pl.ds(w0, wlen), :], q_pad, preferred_element_type=F32)
          + jnp.concatenate([band] * NSA_REP, axis=1))
    e = jnp.exp2(sw - jnp.max(sw, axis=0, keepdims=True))
    ow_ref[...] = (jnp.dot(vwt_ref[0, :, pl.ds(w0, wlen)], e.astype(BF16), preferred_element_type=F32)
                   / jnp.sum(e, axis=0, keepdims=True))

    n_pairs = ((q0 + key_tile - 1) // key_tile + 1) // 2
    init = (m_own, jnp.sum(p, axis=0, keepdims=True), jnp.ones((1, LANES_Q), F32), cm_first)
    _, l, a_last, _ = lax.fori_loop(0, n_pairs, pair, init)
    values(jnp.maximum(2 * n_pairs - 1, 0), p1_ref, a_last)
    o_s = acc_ref[...] / l
    o_w = ow_ref[...]

    gates = _sigmoid(gt_ref[0, :, tok])
    outs = []
    for r in range(NSA_REP):
        sl = slice(r * Q_BLOCK, (r + 1) * Q_BLOCK)
        g_c, g_s, g_w = (gates[r * N_BRANCH + i:r * N_BRANCH + i + 1, :] for i in range(N_BRANCH))
        outs.append(g_c * oct_ref[0, r, :, tok] + g_s * o_s[:, sl] + g_w * o_w[:, sl])
    o_ref[0, tok, :] = jnp.concatenate(outs, axis=0).T.astype(BF16)


def _nsa_main(qt, nsel, k_aug, kw, vs_t, vw_t, oct_, gt, key_tile):
    bsz, _, t = qt.shape
    g = k_aug.shape[1]
    dh = NSA_HEAD_DIM
    ns = t // SLC_BLOCK
    mq = MAIN_BLOCKS * Q_BLOCK
    aw = k_aug.shape[-1]
    assert (t // key_tile) % 2 == 0 and key_tile % Q_BLOCK == 0 and t >= WINDOW + Q_BLOCK and t % mq == 0
    rel = np.arange(2 * WINDOW + Q_BLOCK)[:, None] - WINDOW - np.arange(Q_BLOCK)[None, :]
    band = jnp.asarray(np.where((rel <= 0) & (rel > -WINDOW), 0.0, NEG), dtype=F32)
    return pl.pallas_call(
        functools.partial(_nsa_main_kernel, key_tile=key_tile),
        out_shape=jax.ShapeDtypeStruct((bsz, t, NSA_WIDTH), BF16),
        grid=(bsz, g, t // mq),
        in_specs=[pl.BlockSpec((1, GROUP_Q, mq), lambda b, j, i: (b, j, i)),
                  pl.BlockSpec((1, 1, ns, mq), lambda b, j, i: (b, j, 0, i)),
                  pl.BlockSpec((1, 1, t, aw), lambda b, j, i: (b, j, 0, 0)),
                  pl.BlockSpec((1, dh, t), lambda b, j, i: (b, j, 0)),
                  pl.BlockSpec((1, 1, t, Q_BLOCK), lambda b, j, i: (b, j, 0, 0)),
                  pl.BlockSpec((1, dh, t), lambda b, j, i: (b, j, 0)),
                  pl.BlockSpec((1, NSA_REP, dh, mq), lambda b, j, i: (b, j, 0, i)),
                  pl.BlockSpec((1, GATE_ROWS, mq), lambda b, j, i: (b, j, i)),
                  _resident(band.shape)],
        out_specs=pl.BlockSpec((1, mq, GROUP_Q), lambda b, j, i: (b, i, j)),
        scratch_shapes=[pltpu.VMEM((MAIN_BLOCKS, 2, key_tile, LANES_Q // 2), F32),
                        pltpu.VMEM((MAIN_BLOCKS, 2, key_tile, LANES_Q // 2), F32),
                        pltpu.VMEM((MAIN_BLOCKS, 2, key_tile, LANES_Q // 2), BF16),
                        pltpu.VMEM((MAIN_BLOCKS, 2, key_tile, LANES_Q // 2), BF16),
                        pltpu.VMEM((MAIN_BLOCKS, dh, LANES_Q), F32), pltpu.VMEM((MAIN_BLOCKS, dh, LANES_Q), F32)],
        compiler_params=_params("parallel", "parallel", "arbitrary"),
        name="nsa_main",
    )(qt, nsel, k_aug, vs_t, kw, vw_t, oct_, gt, band)


def _retention_kernel(q_ref, k_ref, v_ref, gate_ref, cos_ref, sin_ref, gng_ref, gnb_ref,
                      o_ref, state_ref, decay_ref, xi_ref, zeta_ref):
    c = RET_CHUNK
    d = RET_HEAD_DIM

    @pl.when(pl.program_id(1) == 0)
    def _():
        state_ref[...] = jnp.zeros_like(state_ref)
        i_row = lax.broadcasted_iota(jnp.int32, (c, c), 0)
        rel = (i_row - lax.broadcasted_iota(jnp.int32, (c, c), 1)).astype(F32)
        pos = i_row.astype(F32)
        for h in range(RET_HEADS):
            log_g = math.log(1.0 - 2.0 ** (-5.0 - h))
            decay_ref[h] = jnp.where(rel >= 0, jnp.exp(jnp.maximum(rel, 0.0) * log_g), 0.0)
            xi_ref[h] = jnp.exp((pos + 1.0) * log_g)
            zeta_ref[h] = jnp.exp((c - 1.0 - pos) * log_g) * (d ** -0.5)

    for sub in range(q_ref.shape[1] // c):
        rows = slice(sub * c, (sub + 1) * c)
        cos = cos_ref[rows, :]
        sin = sin_ref[rows, :]
        for h in range(RET_HEADS):
            log_g = math.log(1.0 - 2.0 ** (-5.0 - h))
            sl = slice(h * d, (h + 1) * d)
            q = q_ref[0, rows, sl]
            k = k_ref[0, rows, sl]
            q = q * cos + pltpu.roll(q, d // 2, 1) * sin
            k = k * cos + pltpu.roll(k, d // 2, 1) * sin
            vb = v_ref[0, rows, sl].astype(BF16)
            inner = lax.dot_general(q.astype(BF16), (k * (d ** -0.5)).astype(BF16), (((1,), (1,)), ((), ())),
                                    preferred_element_type=F32) * decay_ref[h]
            o = jnp.dot(inner.astype(BF16), vb, preferred_element_type=F32)
            state = state_ref[h]
            o = o + jnp.dot((q * xi_ref[h]).astype(BF16), state.astype(BF16), preferred_element_type=F32)
            kv = jnp.dot((k * zeta_ref[h]).T.astype(BF16), vb, preferred_element_type=F32)
            state_ref[h] = math.exp(c * log_g) * state + kv
            mu = jnp.mean(o, axis=-1, keepdims=True)
            oc = o - mu
            var = jnp.mean(oc * oc, axis=-1, keepdims=True)
            y = oc * lax.rsqrt(var + LN_EPS) * gng_ref[:, sl] + gnb_ref[:, sl]
            gate = gate_ref[0, rows, sl]
            o_ref[0, rows, sl] = (gate * _sigmoid(gate) * y).astype(BF16)


def _rotary_tables(t):
    d = RET_HEAD_DIM
    ang = np.arange(t)[:, None] * ROPE_BASE ** (-np.arange(0, d, 2) / d)[None, :]
    cos, sin = np.cos(ang), np.sin(ang)
    return (jnp.asarray(np.concatenate([cos, cos], axis=1), dtype=F32),
            jnp.asarray(np.concatenate([-sin, sin], axis=1), dtype=F32))


def _retention(ret, gn_g, gn_b):
    bsz, t, _ = ret.shape
    c = RET_STEP
    cos2, sin2 = _rotary_tables(t)
    part = lambda j: pl.BlockSpec((1, c, RET_WIDTH), lambda b, n: (b, n, j))
    tab = pl.BlockSpec((c, RET_HEAD_DIM), lambda b, n: (n, 0))
    head_mat = pltpu.VMEM((RET_HEADS, RET_CHUNK, RET_HEAD_DIM), F32)
    return pl.pallas_call(
        _retention_kernel,
        out_shape=jax.ShapeDtypeStruct((bsz, t, RET_WIDTH), BF16),
        grid=(bsz, t // c),
        in_specs=[part(0), part(1), part(2), part(3), tab, tab,
                  _resident((1, RET_WIDTH)), _resident((1, RET_WIDTH))],
        out_specs=pl.BlockSpec((1, c, RET_WIDTH), lambda b, n: (b, n, 0)),
        scratch_shapes=[pltpu.VMEM((RET_HEADS, RET_HEAD_DIM, RET_HEAD_DIM), F32), head_mat, head_mat, head_mat],
        compiler_params=_params("parallel", "arbitrary"),
        name="retention",
    )(ret, ret, ret, ret, cos2, sin2, gn_g.reshape(1, -1), gn_b.reshape(1, -1))


def _tail_kernel(x_ref, on_ref, or_ref, p_ref, wn_ref, wr_ref, w13_ref, w2_ref, wp_ref, wg_ref,
                 g_ref, b_ref, o_ref):
    mix = (jnp.dot(on_ref[...], wn_ref[...], preferred_element_type=F32)
           + jnp.dot(or_ref[...], wr_ref[...], preferred_element_type=F32))
    x = _layer_norm(ALPHA * x_ref[...] + mix, g_ref[0:1, :], b_ref[0:1, :])
    x = _layer_norm(ALPHA * x + 0.5 * _swiglu(x.astype(BF16), w13_ref, w2_ref), g_ref[1:2, :], b_ref[1:2, :])
    e = (jnp.dot(p_ref[...].astype(BF16), wp_ref[...], preferred_element_type=F32)
         * _sigmoid(jnp.dot(x.astype(BF16), wg_ref[...], preferred_element_type=F32)))
    o_ref[...] = _layer_norm(ALPHA * x + e, g_ref[2:3, :], b_ref[2:3, :])


def _tail(x2d, o_nsa, o_ret, p2d, w_out, w13, w2, layer, w_ple, w_gate, g, b):
    n = x2d.shape[0]
    row = lambda w: pl.BlockSpec((ROW_TILE, w), lambda i: (i, 0))
    return pl.pallas_call(
        _tail_kernel,
        out_shape=jax.ShapeDtypeStruct((n, D_MODEL), F32),
        grid=(n // ROW_TILE,),
        in_specs=[row(D_MODEL), row(NSA_WIDTH), row(RET_WIDTH), row(PLE_DIM),
                  _resident((NSA_WIDTH, D_MODEL)), _resident((RET_WIDTH, D_MODEL)),
                  *_ffn_weight_specs(layer, 1),
                  _resident((PLE_DIM, D_MODEL)), _resident((D_MODEL, D_MODEL)),
                  _resident((3, D_MODEL)), _resident((3, D_MODEL))],
        out_specs=row(D_MODEL),
        compiler_params=_params("parallel"),
        name="tail",
    )(x2d, o_nsa, o_ret, p2d, w_out[:NSA_WIDTH].astype(BF16), w_out[NSA_WIDTH:].astype(BF16),
      w13, w2, w_ple.astype(BF16), w_gate.astype(BF16), g, b)


def _token_mixer(x, w_in, cmp_pos, cmp_w1, cmp_w2, gn_g, gn_b):
    t = x.shape[1]
    raw_k, raw_v, ret, k_aug, kw, qt, vs_t, vw_t, gt = _in_proj(x, w_in)
    kc, vct = _compress(raw_k, raw_v, cmp_pos, cmp_w1, cmp_w2)
    oct_, nsel = _nsa_select(qt, kc, vct)
    o_nsa = _nsa_main(qt, nsel, k_aug, kw, vs_t, vw_t, oct_, gt, min(512, t))
    o_ret = _retention(ret, gn_g, gn_b)
    return o_nsa, o_ret


def kernel(x, p, ffn_w13, ffn_w2, w_in, cmp_pos, cmp_w1, cmp_w2, ret_gn_g, ret_gn_b,
           w_out, w_ple, w_ple_gate, ln_g, ln_b):
    bsz, t, d = x.shape
    n = bsz * t
    w13, w2 = ffn_w13.astype(BF16), ffn_w2.astype(BF16)
    for i in range(ffn_w13.shape[0]):
        h = _ffn_ln(x.reshape(n, d), w13, w2, i, 0, ln_g[i, 0], ln_b[i, 0])
        o_nsa, o_ret = _token_mixer(h.reshape(bsz, t, d), w_in[i], cmp_pos[i], cmp_w1[i], cmp_w2[i],
                                    ret_gn_g[i], ret_gn_b[i])
        h = _tail(h, o_nsa.reshape(n, -1), o_ret.reshape(n, -1), p[i].reshape(n, -1), w_out[i],
                  w13, w2, i, w_ple[i], w_ple_gate[i], ln_g[i, 1:], ln_b[i, 1:])
        x = h.reshape(bsz, t, d)
    return x
```

```python
import functools
import math

import jax
import jax.numpy as jnp
import numpy as np
from jax import lax
from jax.experimental import pallas as pl
from jax.experimental.pallas import tpu as pltpu

F32 = jnp.float32
BF16 = jnp.bfloat16

D_MODEL = 1024
D_FF = 2816
NSA_GROUPS = 2
NSA_REP = 4
NSA_HEAD_DIM = 64
N_BRANCH = 3
CMP_BLOCK = 32
CMP_STRIDE = 16
CMP_HIDDEN = 256
SLC_BLOCK = 64
SLC_TOPK = 16
WINDOW = 512
Q_BLOCK = 128
RET_HEADS = 4
RET_HEAD_DIM = 128
RET_CHUNK = 128
RET_STEP = 256
ROPE_BASE = 10000.0
PLE_DIM = 256
DEPTH = 1
ALPHA = (2.0 * DEPTH) ** 0.25
LN_EPS = 1e-5
NEG = -1e30
LOG2_E = math.log2(math.e)

NSA_WIDTH = NSA_GROUPS * NSA_REP * NSA_HEAD_DIM
RET_WIDTH = RET_HEADS * RET_HEAD_DIM
GROUP_Q = NSA_REP * NSA_HEAD_DIM
LANES_Q = NSA_REP * Q_BLOCK
GATE_ROWS = 16
KV_WIDTH = NSA_GROUPS * NSA_HEAD_DIM
TR_Q, TR_VS, TR_VW, TR_G = 0, 512, 640, 768
TR_ROWS = TR_G + NSA_GROUPS * GATE_ROWS
NAT_CMP, NAT_RET = 0, 2 * KV_WIDTH
NAT_KS = NAT_RET + 4 * RET_WIDTH
NAT_KW = NAT_KS + NSA_GROUPS * Q_BLOCK
NAT_COLS = NAT_KW + NSA_GROUPS * Q_BLOCK

KEY_CHUNKS = 1
SUM_ROWS = 16
SEL_BLOCKS = 8
MAIN_BLOCKS = 1
VMEM_LIMIT = 56 * 1024 * 1024
ROW_TILE = 512


def _layer_norm(z, g, b):
    mu = jnp.mean(z, axis=-1, keepdims=True)
    zc = z - mu
    var = jnp.mean(zc * zc, axis=-1, keepdims=True)
    return zc * lax.rsqrt(var + LN_EPS) * g + b


def _sigmoid(v):
    return 1.0 / (1.0 + jnp.exp(-v))


def _params(*sem):
    return pltpu.CompilerParams(dimension_semantics=sem, vmem_limit_bytes=VMEM_LIMIT)


def _resident(shape):
    nd = len(shape)
    return pl.BlockSpec(shape, lambda *_: (0,) * nd, pipeline_mode=pl.Buffered(1))


def _swiglu(xb, w13_ref, w2_ref):
    a = jnp.dot(xb, w13_ref[0, 0, :, :D_FF], preferred_element_type=F32)
    u = jnp.dot(xb, w13_ref[0, 0, :, D_FF:], preferred_element_type=F32)
    h = (a * _sigmoid(a) * u).astype(BF16)
    return jnp.dot(h, w2_ref[0, 0], preferred_element_type=F32)


def _ffn_weight_specs(layer, which):
    pick = lambda *_: (layer, which, 0, 0)
    return [pl.BlockSpec((1, 1, D_MODEL, 2 * D_FF), pick, pipeline_mode=pl.Buffered(1)),
            pl.BlockSpec((1, 1, D_FF, D_MODEL), pick, pipeline_mode=pl.Buffered(1))]


def _ffn_ln_kernel(x_ref, w13_ref, w2_ref, g_ref, b_ref, o_ref):
    x = x_ref[...]
    o_ref[...] = _layer_norm(ALPHA * x + 0.5 * _swiglu(x.astype(BF16), w13_ref, w2_ref), g_ref[...], b_ref[...])


def _ffn_ln(x2d, w13, w2, layer, which, g, b):
    n = x2d.shape[0]
    row = pl.BlockSpec((ROW_TILE, D_MODEL), lambda i: (i, 0))
    return pl.pallas_call(
        _ffn_ln_kernel,
        out_shape=jax.ShapeDtypeStruct((n, D_MODEL), F32),
        grid=(n // ROW_TILE,),
        in_specs=[row, *_ffn_weight_specs(layer, which), _resident((1, D_MODEL)), _resident((1, D_MODEL))],
        out_specs=row,
        compiler_params=_params("parallel"),
        name="ffn_ln",
    )(x2d, w13, w2, g.reshape(1, -1), b.reshape(1, -1))


def _in_proj_kernel(x_ref, wn_ref, wt_ref, ck_ref, cv_ref, ret_ref, ka_ref, kw_ref, qt_ref, vst_ref, vwt_ref,
                    gt_ref):
    xb = x_ref[0].astype(BF16)
    tm = xb.shape[0]
    nat = jnp.dot(xb, wn_ref[...], preferred_element_type=F32)
    ck_ref[0] = nat[:, NAT_CMP:NAT_CMP + KV_WIDTH]
    cv_ref[0] = nat[:, NAT_CMP + KV_WIDTH:NAT_RET]
    ret_ref[0] = nat[:, NAT_RET:NAT_KS]
    ns = ka_ref.shape[3] - Q_BLOCK
    row = pl.program_id(1) * tm + lax.broadcasted_iota(jnp.int32, (tm, ns), 0)
    own = row // SLC_BLOCK == lax.broadcasted_iota(jnp.int32, (tm, ns), 1)
    bias = jnp.where(own, NEG, 0.0).astype(BF16)
    for g in range(NSA_GROUPS):
        ka_ref[0, g, :, 0:Q_BLOCK] = nat[:, NAT_KS + g * Q_BLOCK:NAT_KS + (g + 1) * Q_BLOCK].astype(BF16)
        ka_ref[0, g, :, Q_BLOCK:] = bias
        kw_ref[0, g] = nat[:, NAT_KW + g * Q_BLOCK:NAT_KW + (g + 1) * Q_BLOCK].astype(BF16)
    tr = lax.dot_general(wt_ref[...], xb, (((1,), (1,)), ((), ())), preferred_element_type=F32)
    qt_ref[0] = (tr[TR_Q:TR_VS] * (NSA_HEAD_DIM ** -0.5 * LOG2_E)).astype(BF16)
    vst_ref[0] = tr[TR_VS:TR_VW].astype(BF16)
    vwt_ref[0] = tr[TR_VW:TR_G].astype(BF16)
    gt_ref[0] = tr[TR_G:]


def _split_w_in(w_in):
    c_q = NSA_WIDTH
    c_kv = c_q + 2 * N_BRANCH * KV_WIDTH
    c_g = c_kv + NSA_GROUPS * NSA_REP * N_BRANCH
    kv = w_in[:, c_q:c_kv].reshape(D_MODEL, 2 * N_BRANCH, KV_WIDTH)
    gates = w_in[:, c_kv:c_g].reshape(D_MODEL, NSA_GROUPS, NSA_REP * N_BRANCH)
    gates = jnp.pad(gates, ((0, 0), (0, 0), (0, GATE_ROWS - NSA_REP * N_BRANCH)))
    pad_groups = lambda w: jnp.pad(w.reshape(D_MODEL, NSA_GROUPS, NSA_HEAD_DIM),
                                   ((0, 0), (0, 0), (0, Q_BLOCK - NSA_HEAD_DIM))).reshape(D_MODEL, -1)
    w_nat = jnp.concatenate([kv[:, 0], kv[:, 1], w_in[:, c_g:], pad_groups(kv[:, 2]), pad_groups(kv[:, 4])],
                            axis=1)
    w_tr = jnp.concatenate([w_in[:, :c_q], kv[:, 3], kv[:, 5],
                            gates.reshape(D_MODEL, NSA_GROUPS * GATE_ROWS)], axis=1)
    return w_nat.astype(BF16), w_tr.T.astype(BF16)


def _in_proj(x, w_in):
    bsz, t, _ = x.shape
    g = NSA_GROUPS
    ns = t // SLC_BLOCK
    w_nat, w_trt = _split_w_in(w_in)
    rows = lambda w: pl.BlockSpec((1, ROW_TILE, w), lambda b, i: (b, i, 0))
    cols = lambda h: pl.BlockSpec((1, h, ROW_TILE), lambda b, i: (b, 0, i))
    grouped = lambda w: pl.BlockSpec((1, g, ROW_TILE, w), lambda b, i: (b, 0, i, 0))
    return pl.pallas_call(
        _in_proj_kernel,
        out_shape=(jax.ShapeDtypeStruct((bsz, t, KV_WIDTH), F32),
                   jax.ShapeDtypeStruct((bsz, t, KV_WIDTH), F32),
                   jax.ShapeDtypeStruct((bsz, t, NAT_KS - NAT_RET), F32),
                   jax.ShapeDtypeStruct((bsz, g, t, Q_BLOCK + ns), BF16),
                   jax.ShapeDtypeStruct((bsz, g, t, Q_BLOCK), BF16),
                   jax.ShapeDtypeStruct((bsz, TR_VS - TR_Q, t), BF16),
                   jax.ShapeDtypeStruct((bsz, TR_VW - TR_VS, t), BF16),
                   jax.ShapeDtypeStruct((bsz, TR_G - TR_VW, t), BF16),
                   jax.ShapeDtypeStruct((bsz, TR_ROWS - TR_G, t), F32)),
        grid=(bsz, t // ROW_TILE),
        in_specs=[rows(D_MODEL), _resident((D_MODEL, NAT_COLS)), _resident((TR_ROWS, D_MODEL))],
        out_specs=(rows(KV_WIDTH), rows(KV_WIDTH), rows(NAT_KS - NAT_RET), grouped(Q_BLOCK + ns), grouped(Q_BLOCK),
                   cols(TR_VS - TR_Q), cols(TR_VW - TR_VS), cols(TR_G - TR_VW), cols(TR_ROWS - TR_G)),
        compiler_params=_params("parallel", "parallel"),
        name="in_proj",
    )(x, w_nat, w_trt)


def _gelu_tanh(v):
    return 0.5 * v * (1.0 + jnp.tanh(math.sqrt(2.0 / math.pi) * (v + 0.044715 * (v * v * v))))


def _compress_hidden(h, pos_ref, w1_ref):
    top = jnp.dot((h + pos_ref[0:1, :]).astype(BF16), w1_ref[0], preferred_element_type=F32)
    bot = jnp.dot((h + pos_ref[1:2, :]).astype(BF16), w1_ref[1], preferred_element_type=F32)
    nc = h.shape[0]
    return _gelu_tanh(top + pltpu.roll(bot, nc - 1, 0)).astype(BF16)


def _compress_kernel(rk_ref, rv_ref, pk_ref, pv_ref, w1k_ref, w1v_ref, w2k_ref, w2vt_ref, kc_ref, vct_ref):
    dh = NSA_HEAD_DIM
    nc = kc_ref.shape[2]

    def half_blocks(raw_ref, g):
        rows = [raw_ref[0, pl.ds(l, nc, stride=CMP_STRIDE), :] for l in range(CMP_STRIDE)]
        return jnp.concatenate([r[:, g * dh:(g + 1) * dh] for r in rows], axis=1)

    row = lax.broadcasted_iota(jnp.int32, (nc, dh), 0)
    col = lax.broadcasted_iota(jnp.int32, (dh, nc), 1)
    for g in range(NSA_GROUPS):
        gk = _compress_hidden(half_blocks(rk_ref, g), pk_ref, w1k_ref)
        kc = jnp.dot(gk, w2k_ref[...], preferred_element_type=F32)
        kc_ref[0, g] = jnp.where(row < nc - 1, kc, 0.0).astype(BF16)
        gv = _compress_hidden(half_blocks(rv_ref, g), pv_ref, w1v_ref)
        vct = lax.dot_general(w2vt_ref[...], gv, (((1,), (1,)), ((), ())),
                              preferred_element_type=F32)
        vct_ref[0, g, 0:dh, :] = jnp.where(col < nc - 1, vct, 0.0).astype(BF16)
        vct_ref[0, g, dh:, :] = jnp.ones((SUM_ROWS, nc), BF16)


def _compress(raw_k, raw_v, cmp_pos, cmp_w1, cmp_w2):
    bsz, t, width = raw_k.shape
    g, dh = NSA_GROUPS, NSA_HEAD_DIM
    nc = t // CMP_STRIDE
    half = CMP_BLOCK // 2
    hw = half * dh
    pos = cmp_pos.reshape(2, 2, hw)
    w1 = cmp_w1.reshape(2, 2, hw, CMP_HIDDEN).astype(BF16)
    return pl.pallas_call(
        _compress_kernel,
        out_shape=(jax.ShapeDtypeStruct((bsz, g, nc, dh), BF16),
                   jax.ShapeDtypeStruct((bsz, g, dh + SUM_ROWS, nc), BF16)),
        grid=(bsz,),
        in_specs=[pl.BlockSpec((1, t, width), lambda b: (b, 0, 0)), pl.BlockSpec((1, t, width), lambda b: (b, 0, 0)),
                  _resident((2, hw)), _resident((2, hw)),
                  _resident((2, hw, CMP_HIDDEN)), _resident((2, hw, CMP_HIDDEN)),
                  _resident((CMP_HIDDEN, dh)), _resident((dh, CMP_HIDDEN))],
        out_specs=(pl.BlockSpec((1, g, nc, dh), lambda b: (b, 0, 0, 0)),
                   pl.BlockSpec((1, g, dh + SUM_ROWS, nc), lambda b: (b, 0, 0, 0))),
        compiler_params=_params("parallel"),
        name="compress",
    )(raw_k, raw_v, pos[0], pos[1], w1[0], w1[1], cmp_w2[0].astype(BF16), cmp_w2[1].T.astype(BF16))


def _q_lanes(qt_ref, j=0):
    qt = qt_ref[0, :, j * Q_BLOCK:(j + 1) * Q_BLOCK]
    return jnp.concatenate([qt[r * NSA_HEAD_DIM:(r + 1) * NSA_HEAD_DIM, :] for r in range(NSA_REP)], axis=1)


def _lane_token(shape, q0):
    return q0 + (lax.broadcasted_iota(jnp.int32, shape, 1) & (Q_BLOCK - 1))


def _nsa_select_kernel(qt_ref, kc_ref, vct_ref, ovl_ref, bias_ref, oct_ref, nsel_ref, imp_ref):
    step = pl.program_id(2)
    dh = NSA_HEAD_DIM
    nc = kc_ref.shape[2]
    kc = kc_ref[0, 0]
    lhs = jnp.concatenate([ovl_ref[...], vct_ref[0, 0]], axis=0)
    n_blk = ovl_ref.shape[0]

    def scores(j):
        qb = step * SEL_BLOCKS + j
        s = jnp.dot(kc, _q_lanes(qt_ref, j), preferred_element_type=F32)
        bias = bias_ref[pl.ds(pl.multiple_of(nc - qb * (Q_BLOCK // CMP_STRIDE), 8), nc), :]
        return s + jnp.concatenate([bias] * NSA_REP, axis=1)

    def select(pair):
        lanes = slice(pair * 2 * Q_BLOCK, (pair + 1) * 2 * Q_BLOCK)
        imp = imp_ref[:, lanes]
        blk = lax.broadcasted_iota(jnp.int32, imp.shape, 0)
        t = ((step * SEL_BLOCKS + 2 * pair) * Q_BLOCK
             + lax.broadcasted_iota(jnp.int32, imp.shape, 1))
        cur = t // SLC_BLOCK
        valid = blk <= cur
        forced = (blk == 0) | (blk >= cur - 1)
        val0 = jnp.where(valid & jnp.logical_not(forced), imp, -jnp.inf)
        blk_f = blk.astype(F32)
        val = val0
        for _ in range(max(min(SLC_TOPK, n_blk) - 3, 0)):
            top = jnp.max(val, axis=0, keepdims=True)
            first = jnp.min(jnp.where(val == top, blk_f, float(n_blk)), axis=0, keepdims=True)
            val = jnp.where(blk_f == first, -jnp.inf, val)
            yield
        chosen = valid & (forced | (val != val0))
        nsel_ref[0, 0, :, lanes] = jnp.where(chosen & (blk < 2 * (t // Q_BLOCK)), 0.0, 1.0).astype(BF16)

    running = []

    def advance(n):
        for _ in range(n):
            if running and next(running[0], True):
                running.pop(0)

    s_next = scores(0)
    for j in range(SEL_BLOCKS):
        qb = step * SEL_BLOCKS + j
        s = s_next
        if j + 1 < SEL_BLOCKS:
            s_next = scores(j + 1)
        advance(3)
        e = jnp.exp2(s - jnp.max(s, axis=0, keepdims=True)).astype(BF16)
        advance(3)
        r = jnp.dot(lhs, e, preferred_element_type=F32)
        tok = _lane_token((1, LANES_Q), qb * Q_BLOCK)
        scale = jnp.where(tok >= CMP_BLOCK - 1, 1.0 / r[n_blk + dh:n_blk + dh + 1], 0.0)
        o = r[n_blk:n_blk + dh] * scale
        w = r[0:n_blk] * scale
        imp = w[:, 0:Q_BLOCK]
        for h in range(NSA_REP):
            oct_ref[0, h, :, j * Q_BLOCK:(j + 1) * Q_BLOCK] = o[:, h * Q_BLOCK:(h + 1) * Q_BLOCK]
            if h:
                imp = imp + w[:, h * Q_BLOCK:(h + 1) * Q_BLOCK]
        imp_ref[:, j * Q_BLOCK:(j + 1) * Q_BLOCK] = imp
        advance(3)
        if j % 2 == 1:
            running.append(select(j // 2))
    while running:
        advance(1)


def _cmp_bias_table(t):
    nc = t // CMP_STRIDE
    rel = (np.arange(2 * nc)[:, None] - nc) * CMP_STRIDE + CMP_BLOCK - 1
    return jnp.asarray(np.where(rel <= np.arange(Q_BLOCK)[None, :], 0.0, NEG), dtype=F32)


def _overlap_t(t):
    n_cmp = t // CMP_STRIDE
    n_slc = t // SLC_BLOCK
    cs = np.arange(n_cmp) * CMP_STRIDE
    ss = np.arange(n_slc) * SLC_BLOCK
    ov = np.clip(np.minimum(cs[None, :] + CMP_BLOCK, ss[:, None] + SLC_BLOCK)
                 - np.maximum(cs[None, :], ss[:, None]), 0, None) / CMP_BLOCK
    return jnp.asarray(ov, dtype=BF16)


def _nsa_select(qt, kc, vct):
    bsz, _, t = qt.shape
    g, nc, dh = kc.shape[1:]
    ns = t // SLC_BLOCK
    sq = SEL_BLOCKS * Q_BLOCK
    assert ns % 16 == 0 and t % sq == 0 and SEL_BLOCKS % 2 == 0
    return pl.pallas_call(
        _nsa_select_kernel,
        out_shape=(jax.ShapeDtypeStruct((bsz, g * NSA_REP, dh, t), F32),
                   jax.ShapeDtypeStruct((bsz, g, ns, t), BF16)),
        grid=(bsz, g, t // sq),
        in_specs=[pl.BlockSpec((1, GROUP_Q, sq), lambda b, j, i: (b, j, i)),
                  pl.BlockSpec((1, 1, nc, dh), lambda b, j, i: (b, j, 0, 0)),
                  pl.BlockSpec((1, 1, dh + SUM_ROWS, nc), lambda b, j, i: (b, j, 0, 0)),
                  _resident((ns, nc)), _resident((2 * nc, Q_BLOCK))],
        out_specs=(pl.BlockSpec((1, NSA_REP, dh, sq), lambda b, j, i: (b, j, 0, i)),
                   pl.BlockSpec((1, 1, ns, sq), lambda b, j, i: (b, j, 0, i))),
        scratch_shapes=[pltpu.VMEM((ns, sq), F32)],
        compiler_params=_params("parallel", "parallel", "parallel"),
        name="nsa_select",
    )(qt, kc, vct, _overlap_t(t), _cmp_bias_table(t))


def _nsa_main_kernel(qt_ref, nsel_ref, ka_ref, vst_ref, kw_ref, vwt_ref, oct_ref, gt_ref, band_ref,
                     o_ref, s0_ref, s1_ref, p0_ref, p1_ref, acc_ref, ow_ref, *, key_tile):
    for j in range(MAIN_BLOCKS):
        _nsa_main_block(j, qt_ref, nsel_ref, ka_ref, vst_ref, kw_ref, vwt_ref, oct_ref, gt_ref, band_ref,
                        o_ref, s0_ref.at[j], s1_ref.at[j], p0_ref.at[j], p1_ref.at[j], acc_ref.at[j],
                        ow_ref.at[j], key_tile)


def _nsa_main_block(j, qt_ref, nsel_ref, ka_ref, vst_ref, kw_ref, vwt_ref, oct_ref, gt_ref, band_ref,
                    o_ref, s0_ref, s1_ref, p0_ref, p1_ref, acc_ref, ow_ref, key_tile):
    tok = slice(j * Q_BLOCK, (j + 1) * Q_BLOCK)
    qb = pl.program_id(2) * MAIN_BLOCKS + j
    q0 = qb * Q_BLOCK
    dh = NSA_HEAD_DIM
    last_tile = ka_ref.shape[2] // key_tile - 1
    qm = _q_lanes(qt_ref, j)
    nsel = nsel_ref[0, 0, :, tok]
    q_pad = jnp.concatenate([qm, jnp.zeros((Q_BLOCK - dh, LANES_Q), BF16)], axis=0)
    q_aug = jnp.concatenate([q_pad, jnp.concatenate([nsel] * NSA_REP, axis=1)], axis=0)

    half = LANES_Q // 2
    chunk = key_tile // KEY_CHUNKS
    q_half = [q_aug[:, h * half:(h + 1) * half] for h in range(2)]
    pieces = [(h, c) for c in range(KEY_CHUNKS) for h in range(2)]

    def step(s_in, p_out, col_max, m, l, kt_v, p_in, a_v, kt_n, s_out):
        m_new = jnp.maximum(m, col_max)
        a = jnp.exp2(m - m_new)
        kv0 = pl.multiple_of(kt_v * key_tile, key_tile)
        kn0 = pl.multiple_of(kt_n * key_tile, key_tile)
        sums = [None, None]
        new_max = [None, None]
        acc = [acc_ref[:, h * half:(h + 1) * half] * a_v[:, h * half:(h + 1) * half] for h in range(2)]
        for h, c in pieces:
            lanes = slice(h * half, (h + 1) * half)
            rows = slice(c * chunk, (c + 1) * chunk)
            p = jnp.exp2(s_in[h, rows, :] - m_new[:, lanes])
            p_out[h, rows, :] = p.astype(BF16)
            ps = jnp.sum(p, axis=0, keepdims=True)
            sums[h] = ps if sums[h] is None else sums[h] + ps
            acc[h] = acc[h] + jnp.dot(vst_ref[0, :, pl.ds(kv0 + c * chunk, chunk)], p_in[h, rows, :],
                                      preferred_element_type=F32)
            s = jnp.dot(ka_ref[0, 0, pl.ds(kn0 + c * chunk, chunk), :], q_half[h],
                        preferred_element_type=F32)
            s_out[h, rows, :] = s
            cm = jnp.max(s, axis=0, keepdims=True)
            new_max[h] = cm if new_max[h] is None else jnp.maximum(new_max[h], cm)
        for h in range(2):
            acc_ref[:, h * half:(h + 1) * half] = acc[h]
        return (m_new, l * a + jnp.concatenate(sums, axis=1), a, jnp.concatenate(new_max, axis=1))

    def pair(j, carry):
        m, l, a_prev, cm0 = carry
        t0 = 2 * j
        m, l, a0, cm1 = step(s0_ref, p0_ref, cm0, m, l,
                             jnp.maximum(t0 - 1, 0), p1_ref, a_prev, t0 + 1, s1_ref)
        m, l, a1, cm0 = step(s1_ref, p1_ref, cm1, m, l,
                             t0, p0_ref, a0, jnp.minimum(t0 + 2, last_tile), s0_ref)
        return m, l, a1, cm0

    def scores(kt, s_ref):
        k0 = pl.multiple_of(kt * key_tile, key_tile)
        col_max = []
        for h in range(2):
            s = jnp.dot(ka_ref[0, 0, pl.ds(k0, key_tile), :], q_half[h], preferred_element_type=F32)
            s_ref[h] = s
            col_max.append(jnp.max(s, axis=0, keepdims=True))
        return jnp.concatenate(col_max, axis=1)

    def values(kt, p_ref, a):
        k0 = pl.multiple_of(kt * key_tile, key_tile)
        for h in range(2):
            lanes = slice(h * half, (h + 1) * half)
            acc_ref[:, lanes] = acc_ref[:, lanes] * a[:, lanes] + jnp.dot(
                vst_ref[0, :, pl.ds(k0, key_tile)], p_ref[h], preferred_element_type=F32)

    p1_ref[...] = jnp.zeros_like(p1_ref)
    cm_first = scores(0, s0_ref)

    d0 = pl.multiple_of(q0, Q_BLOCK)
    s = jnp.dot(ka_ref[0, 0, pl.ds(d0, Q_BLOCK), 0:Q_BLOCK], q_pad, preferred_element_type=F32)
    tq = lax.broadcasted_iota(jnp.int32, s.shape, 1) & (Q_BLOCK - 1)
    kl = lax.broadcasted_iota(jnp.int32, s.shape, 0)
    s = jnp.where(kl <= tq, s, NEG)
    m_own = jnp.max(s, axis=0, keepdims=True)
    p = jnp.exp2(s - m_own)
    acc_ref[...] = jnp.dot(vst_ref[0, :, pl.ds(d0, Q_BLOCK)], p.astype(BF16),
                           preferred_element_type=F32)

    wlen = WINDOW + Q_BLOCK
    w0 = pl.multiple_of(jnp.maximum(q0 - WINDOW, 0), Q_BLOCK)
    band = band_ref[pl.ds(pl.multiple_of(WINDOW - (q0 - w0), Q_BLOCK), wlen), :]
    sw = (jnp.dot(kw_ref[0, 0, pl.ds(w0, wlen), :], q_pad, preferred_element_type=F32)
          + jnp.concatenate([band] * NSA_REP, axis=1))
    e = jnp.exp2(sw - jnp.max(sw, axis=0, keepdims=True))
    ow_ref[...] = (jnp.dot(vwt_ref[0, :, pl.ds(w0, wlen)], e.astype(BF16), preferred_element_type=F32)
                   / jnp.sum(e, axis=0, keepdims=True))

    n_tiles = (q0 + key_tile - 1) // key_tile
    n_pairs = n_tiles // 2
    init = (m_own, jnp.sum(p, axis=0, keepdims=True), jnp.ones((1, LANES_Q), F32), cm_first)
    m, l, a_prev, cm_last = lax.fori_loop(0, n_pairs, pair, init)
    kt_prev = jnp.maximum(2 * n_pairs - 1, 0)

    def odd_tail(_):
        m_new = jnp.maximum(m, cm_last)
        a = jnp.exp2(m - m_new)
        sums = []
        for h in range(2):
            ph = jnp.exp2(s0_ref[h] - m_new[:, h * half:(h + 1) * half])
            p0_ref[h] = ph.astype(BF16)
            sums.append(jnp.sum(ph, axis=0, keepdims=True))
        values(kt_prev, p1_ref, a_prev)
        values(2 * n_pairs, p0_ref, a)
        return l * a + jnp.concatenate(sums, axis=1)

    def even_tail(_):
        values(kt_prev, p1_ref, a_prev)
        return l

    l = lax.cond(n_tiles % 2 == 1, odd_tail, even_tail, None)
    o_s = acc_ref[...] / l
    o_w = ow_ref[...]

    gates = _sigmoid(gt_ref[0, :, tok])
    outs = []
    for r in range(NSA_REP):
        sl = slice(r * Q_BLOCK, (r + 1) * Q_BLOCK)
        g_c, g_s, g_w = (gates[r * N_BRANCH + i:r * N_BRANCH + i + 1, :] for i in range(N_BRANCH))
        outs.append(g_c * oct_ref[0, r, :, tok] + g_s * o_s[:, sl] + g_w * o_w[:, sl])
    o_ref[0, tok, :] = jnp.concatenate(outs, axis=0).T.astype(BF16)


def _nsa_main(qt, nsel, k_aug, kw, vs_t, vw_t, oct_, gt, key_tile):
    bsz, _, t = qt.shape
    g = k_aug.shape[1]
    dh = NSA_HEAD_DIM
    ns = t // SLC_BLOCK
    mq = MAIN_BLOCKS * Q_BLOCK
    aw = k_aug.shape[-1]
    assert (t // key_tile) % 2 == 0 and key_tile % Q_BLOCK == 0 and t >= WINDOW + Q_BLOCK and t % mq == 0
    rel = np.arange(2 * WINDOW + Q_BLOCK)[:, None] - WINDOW - np.arange(Q_BLOCK)[None, :]
    band = jnp.asarray(np.where((rel <= 0) & (rel > -WINDOW), 0.0, NEG), dtype=F32)
    return pl.pallas_call(
        functools.partial(_nsa_main_kernel, key_tile=key_tile),
        out_shape=jax.ShapeDtypeStruct((bsz, t, NSA_WIDTH), BF16),
        grid=(bsz, g, t // mq),
        in_specs=[pl.BlockSpec((1, GROUP_Q, mq), lambda b, j, i: (b, j, i)),
                  pl.BlockSpec((1, 1, ns, mq), lambda b, j, i: (b, j, 0, i)),
                  pl.BlockSpec((1, 1, t, aw), lambda b, j, i: (b, j, 0, 0)),
                  pl.BlockSpec((1, dh, t), lambda b, j, i: (b, j, 0)),
                  pl.BlockSpec((1, 1, t, Q_BLOCK), lambda b, j, i: (b, j, 0, 0)),
                  pl.BlockSpec((1, dh, t), lambda b, j, i: (b, j, 0)),
                  pl.BlockSpec((1, NSA_REP, dh, mq), lambda b, j, i: (b, j, 0, i)),
                  pl.BlockSpec((1, GATE_ROWS, mq), lambda b, j, i: (b, j, i)),
                  _resident(band.shape)],
        out_specs=pl.BlockSpec((1, mq, GROUP_Q), lambda b, j, i: (b, i, j)),
        scratch_shapes=[pltpu.VMEM((MAIN_BLOCKS, 2, key_tile, LANES_Q // 2), F32),
                        pltpu.VMEM((MAIN_BLOCKS, 2, key_tile, LANES_Q // 2), F32),
                        pltpu.VMEM((MAIN_BLOCKS, 2, key_tile, LANES_Q // 2), BF16),
                        pltpu.VMEM((MAIN_BLOCKS, 2, key_tile, LANES_Q // 2), BF16),
                        pltpu.VMEM((MAIN_BLOCKS, dh, LANES_Q), F32), pltpu.VMEM((MAIN_BLOCKS, dh, LANES_Q), F32)],
        compiler_params=_params("parallel", "parallel", "arbitrary"),
        name="nsa_main",
    )(qt, nsel, k_aug, vs_t, kw, vw_t, oct_, gt, band)


def _retention_kernel(q_ref, k_ref, v_ref, gate_ref, cos_ref, sin_ref, gng_ref, gnb_ref,
                      o_ref, state_ref, decay_ref, xi_ref, zeta_ref):
    c = RET_CHUNK
    d = RET_HEAD_DIM

    @pl.when(pl.program_id(1) == 0)
    def _():
        state_ref[...] = jnp.zeros_like(state_ref)
        i_row = lax.broadcasted_iota(jnp.int32, (c, c), 0)
        rel = (i_row - lax.broadcasted_iota(jnp.int32, (c, c), 1)).astype(F32)
        pos = i_row.astype(F32)
        for h in range(RET_HEADS):
            log_g = math.log(1.0 - 2.0 ** (-5.0 - h))
            decay_ref[h] = jnp.where(rel >= 0, jnp.exp(jnp.maximum(rel, 0.0) * log_g), 0.0)
            xi_ref[h] = jnp.exp((pos + 1.0) * log_g)
            zeta_ref[h] = jnp.exp((c - 1.0 - pos) * log_g) * (d ** -0.5)

    for sub in range(q_ref.shape[1] // c):
        rows = slice(sub * c, (sub + 1) * c)
        cos = cos_ref[rows, :]
        sin = sin_ref[rows, :]
        for h in range(RET_HEADS):
            log_g = math.log(1.0 - 2.0 ** (-5.0 - h))
            sl = slice(h * d, (h + 1) * d)
            q = q_ref[0, rows, sl]
            k = k_ref[0, rows, sl]
            q = q * cos + pltpu.roll(q, d // 2, 1) * sin
            k = k * cos + pltpu.roll(k, d // 2, 1) * sin
            vb = v_ref[0, rows, sl].astype(BF16)
            inner = lax.dot_general(q.astype(BF16), (k * (d ** -0.5)).astype(BF16), (((1,), (1,)), ((), ())),
                                    preferred_element_type=F32) * decay_ref[h]
            o = jnp.dot(inner.astype(BF16), vb, preferred_element_type=F32)
            state = state_ref[h]
            o = o + jnp.dot((q * xi_ref[h]).astype(BF16), state.astype(BF16), preferred_element_type=F32)
            kv = jnp.dot((k * zeta_ref[h]).T.astype(BF16), vb, preferred_element_type=F32)
            state_ref[h] = math.exp(c * log_g) * state + kv
            mu = jnp.mean(o, axis=-1, keepdims=True)
            oc = o - mu
            var = jnp.mean(oc * oc, axis=-1, keepdims=True)
            y = oc * lax.rsqrt(var + LN_EPS) * gng_ref[:, sl] + gnb_ref[:, sl]
            gate = gate_ref[0, rows, sl]
            o_ref[0, rows, sl] = (gate * _sigmoid(gate) * y).astype(BF16)


def _rotary_tables(t):
    d = RET_HEAD_DIM
    ang = np.arange(t)[:, None] * ROPE_BASE ** (-np.arange(0, d, 2) / d)[None, :]
    cos, sin = np.cos(ang), np.sin(ang)
    return (jnp.asarray(np.concatenate([cos, cos], axis=1), dtype=F32),
            jnp.asarray(np.concatenate([-sin, sin], axis=1), dtype=F32))


def _retention(ret, gn_g, gn_b):
    bsz, t, _ = ret.shape
    c = RET_STEP
    cos2, sin2 = _rotary_tables(t)
    part = lambda j: pl.BlockSpec((1, c, RET_WIDTH), lambda b, n: (b, n, j))
    tab = pl.BlockSpec((c, RET_HEAD_DIM), lambda b, n: (n, 0))
    head_mat = pltpu.VMEM((RET_HEADS, RET_CHUNK, RET_HEAD_DIM), F32)
    return pl.pallas_call(
        _retention_kernel,
        out_shape=jax.ShapeDtypeStruct((bsz, t, RET_WIDTH), BF16),
        grid=(bsz, t // c),
        in_specs=[part(0), part(1), part(2), part(3), tab, tab,
                  _resident((1, RET_WIDTH)), _resident((1, RET_WIDTH))],
        out_specs=pl.BlockSpec((1, c, RET_WIDTH), lambda b, n: (b, n, 0)),
        scratch_shapes=[pltpu.VMEM((RET_HEADS, RET_HEAD_DIM, RET_HEAD_DIM), F32), head_mat, head_mat, head_mat],
        compiler_params=_params("parallel", "arbitrary"),
        name="retention",
    )(ret, ret, ret, ret, cos2, sin2, gn_g.reshape(1, -1), gn_b.reshape(1, -1))


def _tail_kernel(x_ref, on_ref, or_ref, p_ref, wn_ref, wr_ref, w13_ref, w2_ref, wp_ref, wg_ref,
                 g_ref, b_ref, o_ref):
    mix = (jnp.dot(on_ref[...], wn_ref[...], preferred_element_type=F32)
           + jnp.dot(or_ref[...], wr_ref[...], preferred_element_type=F32))
    x = _layer_norm(ALPHA * x_ref[...] + mix, g_ref[0:1, :], b_ref[0:1, :])
    x = _layer_norm(ALPHA * x + 0.5 * _swiglu(x.astype(BF16), w13_ref, w2_ref), g_ref[1:2, :], b_ref[1:2, :])
    e = (jnp.dot(p_ref[...].astype(BF16), wp_ref[...], preferred_element_type=F32)
         * _sigmoid(jnp.dot(x.astype(BF16), wg_ref[...], preferred_element_type=F32)))
    o_ref[...] = _layer_norm(ALPHA * x + e, g_ref[2:3, :], b_ref[2:3, :])


def _tail(x2d, o_nsa, o_ret, p2d, w_out, w13, w2, layer, w_ple, w_gate, g, b):
    n = x2d.shape[0]
    row = lambda w: pl.BlockSpec((ROW_TILE, w), lambda i: (i, 0))
    return pl.pallas_call(
        _tail_kernel,
        out_shape=jax.ShapeDtypeStruct((n, D_MODEL), F32),
        grid=(n // ROW_TILE,),
        in_specs=[row(D_MODEL), row(NSA_WIDTH), row(RET_WIDTH), row(PLE_DIM),
                  _resident((NSA_WIDTH, D_MODEL)), _resident((RET_WIDTH, D_MODEL)),
                  *_ffn_weight_specs(layer, 1),
                  _resident((PLE_DIM, D_MODEL)), _resident((D_MODEL, D_MODEL)),
                  _resident((3, D_MODEL)), _resident((3, D_MODEL))],
        out_specs=row(D_MODEL),
        compiler_params=_params("parallel"),
        name="tail",
    )(x2d, o_nsa, o_ret, p2d, w_out[:NSA_WIDTH].astype(BF16), w_out[NSA_WIDTH:].astype(BF16),
      w13, w2, w_ple.astype(BF16), w_gate.astype(BF16), g, b)


def _token_mixer(x, w_in, cmp_pos, cmp_w1, cmp_w2, gn_g, gn_b):
    t = x.shape[1]
    raw_k, raw_v, ret, k_aug, kw, qt, vs_t, vw_t, gt = _in_proj(x, w_in)
    kc, vct = _compress(raw_k, raw_v, cmp_pos, cmp_w1, cmp_w2)
    oct_, nsel = _nsa_select(qt, kc, vct)
    o_nsa = _nsa_main(qt, nsel, k_aug, kw, vs_t, vw_t, oct_, gt, min(512, t))
    o_ret = _retention(ret, gn_g, gn_b)
    return o_nsa, o_ret


def kernel(x, p, ffn_w13, ffn_w2, w_in, cmp_pos, cmp_w1, cmp_w2, ret_gn_g, ret_gn_b,
           w_out, w_ple, w_ple_gate, ln_g, ln_b):
    bsz, t, d = x.shape
    n = bsz * t
    w13, w2 = ffn_w13.astype(BF16), ffn_w2.astype(BF16)
    for i in range(ffn_w13.shape[0]):
        h = _ffn_ln(x.reshape(n, d), w13, w2, i, 0, ln_g[i, 0], ln_b[i, 0])
        o_nsa, o_ret = _token_mixer(h.reshape(bsz, t, d), w_in[i], cmp_pos[i], cmp_w1[i], cmp_w2[i],
                                    ret_gn_g[i], ret_gn_b[i])
        h = _tail(h, o_nsa.reshape(n, -1), o_ret.reshape(n, -1), p[i].reshape(n, -1), w_out[i],
                  w13, w2, i, w_ple[i], w_ple_gate[i], ln_g[i, 1:], ln_b[i, 1:])
        x = h.reshape(bsz, t, d)
    return x
```

```python
import functools
import math

import jax
import jax.numpy as jnp
import numpy as np
from jax import lax
from jax.experimental import pallas as pl
from jax.experimental.pallas import tpu as pltpu

F32 = jnp.float32
BF16 = jnp.bfloat16

D_MODEL = 1024
D_FF = 2816
NSA_GROUPS = 2
NSA_REP = 4
NSA_HEAD_DIM = 64
N_BRANCH = 3
CMP_BLOCK = 32
CMP_STRIDE = 16
CMP_HIDDEN = 256
SLC_BLOCK = 64
SLC_TOPK = 16
WINDOW = 512
Q_BLOCK = 128
RET_HEADS = 4
RET_HEAD_DIM = 128
RET_CHUNK = 128
RET_STEP = 256
ROPE_BASE = 10000.0
PLE_DIM = 256
DEPTH = 1
ALPHA = (2.0 * DEPTH) ** 0.25
LN_EPS = 1e-5
NEG = -1e30
LOG2_E = math.log2(math.e)

NSA_WIDTH = NSA_GROUPS * NSA_REP * NSA_HEAD_DIM
RET_WIDTH = RET_HEADS * RET_HEAD_DIM
GROUP_Q = NSA_REP * NSA_HEAD_DIM
LANES_Q = NSA_REP * Q_BLOCK
GATE_ROWS = 16
KV_WIDTH = NSA_GROUPS * NSA_HEAD_DIM
TR_Q, TR_VS, TR_VW, TR_G = 0, 512, 640, 768
TR_ROWS = TR_G + NSA_GROUPS * GATE_ROWS
NAT_CMP, NAT_RET = 0, 2 * KV_WIDTH
NAT_KS = NAT_RET + 4 * RET_WIDTH
NAT_KW = NAT_KS + NSA_GROUPS * Q_BLOCK
NAT_COLS = NAT_KW + NSA_GROUPS * Q_BLOCK

KEY_CHUNKS = 1
SUM_ROWS = 16
SEL_BLOCKS = 8
SEL_VARIANTS = 4
MAIN_BLOCKS = 1
VMEM_LIMIT = 56 * 1024 * 1024
ROW_TILE = 512


def _layer_norm(z, g, b):
    mu = jnp.mean(z, axis=-1, keepdims=True)
    zc = z - mu
    var = jnp.mean(zc * zc, axis=-1, keepdims=True)
    return zc * lax.rsqrt(var + LN_EPS) * g + b


def _sigmoid(v):
    return 1.0 / (1.0 + jnp.exp(-v))


def _params(*sem):
    return pltpu.CompilerParams(dimension_semantics=sem, vmem_limit_bytes=VMEM_LIMIT)


def _resident(shape):
    nd = len(shape)
    return pl.BlockSpec(shape, lambda *_: (0,) * nd, pipeline_mode=pl.Buffered(1))


def _swiglu(xb, w13_ref, w2_ref):
    a = jnp.dot(xb, w13_ref[0, 0, :, :D_FF], preferred_element_type=F32)
    u = jnp.dot(xb, w13_ref[0, 0, :, D_FF:], preferred_element_type=F32)
    h = (a * _sigmoid(a) * u).astype(BF16)
    return jnp.dot(h, w2_ref[0, 0], preferred_element_type=F32)


def _ffn_weight_specs(layer, which):
    pick = lambda *_: (layer, which, 0, 0)
    return [pl.BlockSpec((1, 1, D_MODEL, 2 * D_FF), pick, pipeline_mode=pl.Buffered(1)),
            pl.BlockSpec((1, 1, D_FF, D_MODEL), pick, pipeline_mode=pl.Buffered(1))]


def _ffn_ln_kernel(x_ref, w13_ref, w2_ref, g_ref, b_ref, o_ref):
    x = x_ref[...]
    o_ref[...] = _layer_norm(ALPHA * x + 0.5 * _swiglu(x.astype(BF16), w13_ref, w2_ref), g_ref[...], b_ref[...])


def _ffn_ln(x2d, w13, w2, layer, which, g, b):
    n = x2d.shape[0]
    row = pl.BlockSpec((ROW_TILE, D_MODEL), lambda i: (i, 0))
    return pl.pallas_call(
        _ffn_ln_kernel,
        out_shape=jax.ShapeDtypeStruct((n, D_MODEL), F32),
        grid=(n // ROW_TILE,),
        in_specs=[row, *_ffn_weight_specs(layer, which), _resident((1, D_MODEL)), _resident((1, D_MODEL))],
        out_specs=row,
        compiler_params=_params("parallel"),
        name="ffn_ln",
    )(x2d, w13, w2, g.reshape(1, -1), b.reshape(1, -1))


def _in_proj_kernel(x_ref, wn_ref, wt_ref, ck_ref, cv_ref, ret_ref, ka_ref, kw_ref, qt_ref, vst_ref, vwt_ref,
                    gt_ref):
    xb = x_ref[0].astype(BF16)
    tm = xb.shape[0]
    nat = jnp.dot(xb, wn_ref[...], preferred_element_type=F32)
    ck_ref[0] = nat[:, NAT_CMP:NAT_CMP + KV_WIDTH]
    cv_ref[0] = nat[:, NAT_CMP + KV_WIDTH:NAT_RET]
    ret_ref[0] = nat[:, NAT_RET:NAT_KS]
    ns = ka_ref.shape[3] - Q_BLOCK
    row = pl.program_id(1) * tm + lax.broadcasted_iota(jnp.int32, (tm, ns), 0)
    own = row // SLC_BLOCK == lax.broadcasted_iota(jnp.int32, (tm, ns), 1)
    bias = jnp.where(own, NEG, 0.0).astype(BF16)
    for g in range(NSA_GROUPS):
        ka_ref[0, g, :, 0:Q_BLOCK] = nat[:, NAT_KS + g * Q_BLOCK:NAT_KS + (g + 1) * Q_BLOCK].astype(BF16)
        ka_ref[0, g, :, Q_BLOCK:] = bias
        kw_ref[0, g] = nat[:, NAT_KW + g * Q_BLOCK:NAT_KW + (g + 1) * Q_BLOCK].astype(BF16)
    tr = lax.dot_general(wt_ref[...], xb, (((1,), (1,)), ((), ())), preferred_element_type=F32)
    qt_ref[0] = (tr[TR_Q:TR_VS] * (NSA_HEAD_DIM ** -0.5 * LOG2_E)).astype(BF16)
    vst_ref[0] = tr[TR_VS:TR_VW].astype(BF16)
    vwt_ref[0] = tr[TR_VW:TR_G].astype(BF16)
    gt_ref[0] = tr[TR_G:]


def _split_w_in(w_in):
    c_q = NSA_WIDTH
    c_kv = c_q + 2 * N_BRANCH * KV_WIDTH
    c_g = c_kv + NSA_GROUPS * NSA_REP * N_BRANCH
    kv = w_in[:, c_q:c_kv].reshape(D_MODEL, 2 * N_BRANCH, KV_WIDTH)
    gates = w_in[:, c_kv:c_g].reshape(D_MODEL, NSA_GROUPS, NSA_REP * N_BRANCH)
    gates = jnp.pad(gates, ((0, 0), (0, 0), (0, GATE_ROWS - NSA_REP * N_BRANCH)))
    pad_groups = lambda w: jnp.pad(w.reshape(D_MODEL, NSA_GROUPS, NSA_HEAD_DIM),
                                   ((0, 0), (0, 0), (0, Q_BLOCK - NSA_HEAD_DIM))).reshape(D_MODEL, -1)
    w_nat = jnp.concatenate([kv[:, 0], kv[:, 1], w_in[:, c_g:], pad_groups(kv[:, 2]), pad_groups(kv[:, 4])],
                            axis=1)
    w_tr = jnp.concatenate([w_in[:, :c_q], kv[:, 3], kv[:, 5],
                            gates.reshape(D_MODEL, NSA_GROUPS * GATE_ROWS)], axis=1)
    return w_nat.astype(BF16), w_tr.T.astype(BF16)


def _in_proj(x, w_in):
    bsz, t, _ = x.shape
    g = NSA_GROUPS
    ns = t // SLC_BLOCK
    w_nat, w_trt = _split_w_in(w_in)
    rows = lambda w: pl.BlockSpec((1, ROW_TILE, w), lambda b, i: (b, i, 0))
    cols = lambda h: pl.BlockSpec((1, h, ROW_TILE), lambda b, i: (b, 0, i))
    grouped = lambda w: pl.BlockSpec((1, g, ROW_TILE, w), lambda b, i: (b, 0, i, 0))
    return pl.pallas_call(
        _in_proj_kernel,
        out_shape=(jax.ShapeDtypeStruct((bsz, t, KV_WIDTH), F32),
                   jax.ShapeDtypeStruct((bsz, t, KV_WIDTH), F32),
                   jax.ShapeDtypeStruct((bsz, t, NAT_KS - NAT_RET), F32),
                   jax.ShapeDtypeStruct((bsz, g, t, Q_BLOCK + ns), BF16),
                   jax.ShapeDtypeStruct((bsz, g, t, Q_BLOCK), BF16),
                   jax.ShapeDtypeStruct((bsz, TR_VS - TR_Q, t), BF16),
                   jax.ShapeDtypeStruct((bsz, TR_VW - TR_VS, t), BF16),
                   jax.ShapeDtypeStruct((bsz, TR_G - TR_VW, t), BF16),
                   jax.ShapeDtypeStruct((bsz, TR_ROWS - TR_G, t), F32)),
        grid=(bsz, t // ROW_TILE),
        in_specs=[rows(D_MODEL), _resident((D_MODEL, NAT_COLS)), _resident((TR_ROWS, D_MODEL))],
        out_specs=(rows(KV_WIDTH), rows(KV_WIDTH), rows(NAT_KS - NAT_RET), grouped(Q_BLOCK + ns), grouped(Q_BLOCK),
                   cols(TR_VS - TR_Q), cols(TR_VW - TR_VS), cols(TR_G - TR_VW), cols(TR_ROWS - TR_G)),
        compiler_params=_params("parallel", "parallel"),
        name="in_proj",
    )(x, w_nat, w_trt)


def _gelu_tanh(v):
    return 0.5 * v * (1.0 + jnp.tanh(math.sqrt(2.0 / math.pi) * (v + 0.044715 * (v * v * v))))


def _compress_hidden(h, pos_ref, w1_ref):
    top = jnp.dot((h + pos_ref[0:1, :]).astype(BF16), w1_ref[0], preferred_element_type=F32)
    bot = jnp.dot((h + pos_ref[1:2, :]).astype(BF16), w1_ref[1], preferred_element_type=F32)
    nc = h.shape[0]
    return _gelu_tanh(top + pltpu.roll(bot, nc - 1, 0)).astype(BF16)


def _compress_kernel(rk_ref, rv_ref, pk_ref, pv_ref, w1k_ref, w1v_ref, w2k_ref, w2vt_ref, kc_ref, vct_ref):
    dh = NSA_HEAD_DIM
    nc = kc_ref.shape[2]

    def half_blocks(raw_ref, g):
        rows = [raw_ref[0, pl.ds(l, nc, stride=CMP_STRIDE), :] for l in range(CMP_STRIDE)]
        return jnp.concatenate([r[:, g * dh:(g + 1) * dh] for r in rows], axis=1)

    row = lax.broadcasted_iota(jnp.int32, (nc, dh), 0)
    col = lax.broadcasted_iota(jnp.int32, (dh, nc), 1)
    for g in range(NSA_GROUPS):
        gk = _compress_hidden(half_blocks(rk_ref, g), pk_ref, w1k_ref)
        kc = jnp.dot(gk, w2k_ref[...], preferred_element_type=F32)
        kc_ref[0, g] = jnp.where(row < nc - 1, kc, 0.0).astype(BF16)
        gv = _compress_hidden(half_blocks(rv_ref, g), pv_ref, w1v_ref)
        vct = lax.dot_general(w2vt_ref[...], gv, (((1,), (1,)), ((), ())),
                              preferred_element_type=F32)
        vct_ref[0, g, 0:dh, :] = jnp.where(col < nc - 1, vct, 0.0).astype(BF16)
        vct_ref[0, g, dh:, :] = jnp.ones((SUM_ROWS, nc), BF16)


def _compress(raw_k, raw_v, cmp_pos, cmp_w1, cmp_w2):
    bsz, t, width = raw_k.shape
    g, dh = NSA_GROUPS, NSA_HEAD_DIM
    nc = t // CMP_STRIDE
    half = CMP_BLOCK // 2
    hw = half * dh
    pos = cmp_pos.reshape(2, 2, hw)
    w1 = cmp_w1.reshape(2, 2, hw, CMP_HIDDEN).astype(BF16)
    return pl.pallas_call(
        _compress_kernel,
        out_shape=(jax.ShapeDtypeStruct((bsz, g, nc, dh), BF16),
                   jax.ShapeDtypeStruct((bsz, g, dh + SUM_ROWS, nc), BF16)),
        grid=(bsz,),
        in_specs=[pl.BlockSpec((1, t, width), lambda b: (b, 0, 0)), pl.BlockSpec((1, t, width), lambda b: (b, 0, 0)),
                  _resident((2, hw)), _resident((2, hw)),
                  _resident((2, hw, CMP_HIDDEN)), _resident((2, hw, CMP_HIDDEN)),
                  _resident((CMP_HIDDEN, dh)), _resident((dh, CMP_HIDDEN))],
        out_specs=(pl.BlockSpec((1, g, nc, dh), lambda b: (b, 0, 0, 0)),
                   pl.BlockSpec((1, g, dh + SUM_ROWS, nc), lambda b: (b, 0, 0, 0))),
        compiler_params=_params("parallel"),
        name="compress",
    )(raw_k, raw_v, pos[0], pos[1], w1[0], w1[1], cmp_w2[0].astype(BF16), cmp_w2[1].T.astype(BF16))


def _q_lanes(qt_ref, j=0):
    qt = qt_ref[0, :, j * Q_BLOCK:(j + 1) * Q_BLOCK]
    return jnp.concatenate([qt[r * NSA_HEAD_DIM:(r + 1) * NSA_HEAD_DIM, :] for r in range(NSA_REP)], axis=1)


def _lane_token(shape, q0):
    return q0 + (lax.broadcasted_iota(jnp.int32, shape, 1) & (Q_BLOCK - 1))


def _nsa_select_kernel(qt_ref, kc_ref, vct_ref, ovl_ref, bias_ref, oct_ref, nsel_ref, imp_ref):
    step = pl.program_id(2)
    nc = kc_ref.shape[2]
    last_block = ((step + 1) * SEL_BLOCKS * Q_BLOCK - CMP_BLOCK) // CMP_STRIDE
    quarter = nc // SEL_VARIANTS
    branches = [functools.partial(_nsa_select_rows, (v + 1) * quarter, qt_ref, kc_ref, vct_ref, ovl_ref, bias_ref,
                                  oct_ref, nsel_ref, imp_ref) for v in range(SEL_VARIANTS)]
    lax.switch(jnp.minimum(last_block // quarter, SEL_VARIANTS - 1), branches)


def _nsa_select_rows(nc, qt_ref, kc_ref, vct_ref, ovl_ref, bias_ref, oct_ref, nsel_ref, imp_ref):
    step = pl.program_id(2)
    dh = NSA_HEAD_DIM
    nc_all = kc_ref.shape[2]
    kc = kc_ref[0, 0, 0:nc, :]
    lhs = jnp.concatenate([ovl_ref[:, 0:nc], vct_ref[0, 0, :, 0:nc]], axis=0)
    n_blk = ovl_ref.shape[0]

    def scores(j):
        qb = step * SEL_BLOCKS + j
        s = jnp.dot(kc, _q_lanes(qt_ref, j), preferred_element_type=F32)
        bias = bias_ref[pl.ds(pl.multiple_of(nc_all - qb * (Q_BLOCK // CMP_STRIDE), 8), nc), :]
        return s + jnp.concatenate([bias] * NSA_REP, axis=1)

    def select(pair):
        lanes = slice(pair * 2 * Q_BLOCK, (pair + 1) * 2 * Q_BLOCK)
        imp = imp_ref[:, lanes]
        blk = lax.broadcasted_iota(jnp.int32, imp.shape, 0)
        t = ((step * SEL_BLOCKS + 2 * pair) * Q_BLOCK
             + lax.broadcasted_iota(jnp.int32, imp.shape, 1))
        cur = t // SLC_BLOCK
        valid = blk <= cur
        forced = (blk == 0) | (blk >= cur - 1)
        val0 = jnp.where(valid & jnp.logical_not(forced), imp, -jnp.inf)
        blk_f = blk.astype(F32)
        val = val0
        for _ in range(max(min(SLC_TOPK, n_blk) - 3, 0)):
            top = jnp.max(val, axis=0, keepdims=True)
            first = jnp.min(jnp.where(val == top, blk_f, float(n_blk)), axis=0, keepdims=True)
            val = jnp.where(blk_f == first, -jnp.inf, val)
            yield
        chosen = valid & (forced | (val != val0))
        nsel_ref[0, 0, :, lanes] = jnp.where(chosen & (blk < 2 * (t // Q_BLOCK)), 0.0, 1.0).astype(BF16)

    running = []

    def advance(n):
        for _ in range(n):
            if running and next(running[0], True):
                running.pop(0)

    s_next = scores(0)
    for j in range(SEL_BLOCKS):
        qb = step * SEL_BLOCKS + j
        s = s_next
        if j + 1 < SEL_BLOCKS:
            s_next = scores(j + 1)
        advance(3)
        e = jnp.exp2(s - jnp.max(s, axis=0, keepdims=True)).astype(BF16)
        advance(3)
        r = jnp.dot(lhs, e, preferred_element_type=F32)
        tok = _lane_token((1, LANES_Q), qb * Q_BLOCK)
        scale = jnp.where(tok >= CMP_BLOCK - 1, 1.0 / r[n_blk + dh:n_blk + dh + 1], 0.0)
        o = r[n_blk:n_blk + dh] * scale
        w = r[0:n_blk] * scale
        imp = w[:, 0:Q_BLOCK]
        for h in range(NSA_REP):
            oct_ref[0, h, :, j * Q_BLOCK:(j + 1) * Q_BLOCK] = o[:, h * Q_BLOCK:(h + 1) * Q_BLOCK]
            if h:
                imp = imp + w[:, h * Q_BLOCK:(h + 1) * Q_BLOCK]
        imp_ref[:, j * Q_BLOCK:(j + 1) * Q_BLOCK] = imp
        advance(3)
        if j % 2 == 1:
            running.append(select(j // 2))
    while running:
        advance(1)


def _cmp_bias_table(t):
    nc = t // CMP_STRIDE
    rel = (np.arange(2 * nc)[:, None] - nc) * CMP_STRIDE + CMP_BLOCK - 1
    return jnp.asarray(np.where(rel <= np.arange(Q_BLOCK)[None, :], 0.0, NEG), dtype=F32)


def _overlap_t(t):
    n_cmp = t // CMP_STRIDE
    n_slc = t // SLC_BLOCK
    cs = np.arange(n_cmp) * CMP_STRIDE
    ss = np.arange(n_slc) * SLC_BLOCK
    ov = np.clip(np.minimum(cs[None, :] + CMP_BLOCK, ss[:, None] + SLC_BLOCK)
                 - np.maximum(cs[None, :], ss[:, None]), 0, None) / CMP_BLOCK
    return jnp.asarray(ov, dtype=BF16)


def _nsa_select(qt, kc, vct):
    bsz, _, t = qt.shape
    g, nc, dh = kc.shape[1:]
    ns = t // SLC_BLOCK
    sq = SEL_BLOCKS * Q_BLOCK
    assert ns % 16 == 0 and t % sq == 0 and SEL_BLOCKS % 2 == 0
    return pl.pallas_call(
        _nsa_select_kernel,
        out_shape=(jax.ShapeDtypeStruct((bsz, g * NSA_REP, dh, t), F32),
                   jax.ShapeDtypeStruct((bsz, g, ns, t), BF16)),
        grid=(bsz, g, t // sq),
        in_specs=[pl.BlockSpec((1, GROUP_Q, sq), lambda b, j, i: (b, j, i)),
                  pl.BlockSpec((1, 1, nc, dh), lambda b, j, i: (b, j, 0, 0)),
                  pl.BlockSpec((1, 1, dh + SUM_ROWS, nc), lambda b, j, i: (b, j, 0, 0)),
                  _resident((ns, nc)), _resident((2 * nc, Q_BLOCK))],
        out_specs=(pl.BlockSpec((1, NSA_REP, dh, sq), lambda b, j, i: (b, j, 0, i)),
                   pl.BlockSpec((1, 1, ns, sq), lambda b, j, i: (b, j, 0, i))),
        scratch_shapes=[pltpu.VMEM((ns, sq), F32)],
        compiler_params=_params("parallel", "parallel", "parallel"),
        name="nsa_select",
    )(qt, kc, vct, _overlap_t(t), _cmp_bias_table(t))


def _nsa_main_kernel(qt_ref, nsel_ref, ka_ref, vst_ref, kw_ref, vwt_ref, oct_ref, gt_ref, band_ref,
                     o_ref, s0_ref, s1_ref, p0_ref, p1_ref, acc_ref, ow_ref, *, key_tile):
    for j in range(MAIN_BLOCKS):
        _nsa_main_block(j, qt_ref, nsel_ref, ka_ref, vst_ref, kw_ref, vwt_ref, oct_ref, gt_ref, band_ref,
                        o_ref, s0_ref.at[j], s1_ref.at[j], p0_ref.at[j], p1_ref.at[j], acc_ref.at[j],
                        ow_ref.at[j], key_tile)


def _nsa_main_block(j, qt_ref, nsel_ref, ka_ref, vst_ref, kw_ref, vwt_ref, oct_ref, gt_ref, band_ref,
                    o_ref, s0_ref, s1_ref, p0_ref, p1_ref, acc_ref, ow_ref, key_tile):
    tok = slice(j * Q_BLOCK, (j + 1) * Q_BLOCK)
    qb = pl.program_id(2) * MAIN_BLOCKS + j
    q0 = qb * Q_BLOCK
    dh = NSA_HEAD_DIM
    last_tile = ka_ref.shape[2] // key_tile - 1
    qm = _q_lanes(qt_ref, j)
    nsel = nsel_ref[0, 0, :, tok]
    q_pad = jnp.concatenate([qm, jnp.zeros((Q_BLOCK - dh, LANES_Q), BF16)], axis=0)
    q_aug = jnp.concatenate([q_pad, jnp.concatenate([nsel] * NSA_REP, axis=1)], axis=0)

    half = LANES_Q // 2
    chunk = key_tile // KEY_CHUNKS
    q_half = [q_aug[:, h * half:(h + 1) * half] for h in range(2)]
    pieces = [(h, c) for c in range(KEY_CHUNKS) for h in range(2)]

    def step(s_in, p_out, col_max, m, l, kt_v, p_in, a_v, kt_n, s_out):
        m_new = jnp.maximum(m, col_max)
        a = jnp.exp2(m - m_new)
        kv0 = pl.multiple_of(kt_v * key_tile, key_tile)
        kn0 = pl.multiple_of(kt_n * key_tile, key_tile)
        sums = [None, None]
        new_max = [None, None]
        acc = [acc_ref[:, h * half:(h + 1) * half] * a_v[:, h * half:(h + 1) * half] for h in range(2)]
        for h, c in pieces:
            lanes = slice(h * half, (h + 1) * half)
            rows = slice(c * chunk, (c + 1) * chunk)
            p = jnp.exp2(s_in[h, rows, :] - m_new[:, lanes])
            p_out[h, rows, :] = p.astype(BF16)
            ps = jnp.sum(p, axis=0, keepdims=True)
            sums[h] = ps if sums[h] is None else sums[h] + ps
            acc[h] = acc[h] + jnp.dot(vst_ref[0, :, pl.ds(kv0 + c * chunk, chunk)], p_in[h, rows, :],
                                      preferred_element_type=F32)
            s = jnp.dot(ka_ref[0, 0, pl.ds(kn0 + c * chunk, chunk), :], q_half[h],
                        preferred_element_type=F32)
            s_out[h, rows, :] = s
            cm = jnp.max(s, axis=0, keepdims=True)
            new_max[h] = cm if new_max[h] is None else jnp.maximum(new_max[h], cm)
        for h in range(2):
            acc_ref[:, h * half:(h + 1) * half] = acc[h]
        return (m_new, l * a + jnp.concatenate(sums, axis=1), a, jnp.concatenate(new_max, axis=1))

    def pair(j, carry):
        m, l, a_prev, cm0 = carry
        t0 = 2 * j
        m, l, a0, cm1 = step(s0_ref, p0_ref, cm0, m, l,
                             jnp.maximum(t0 - 1, 0), p1_ref, a_prev, t0 + 1, s1_ref)
        m, l, a1, cm0 = step(s1_ref, p1_ref, cm1, m, l,
                             t0, p0_ref, a0, jnp.minimum(t0 + 2, last_tile), s0_ref)
        return m, l, a1, cm0

    def scores(kt, s_ref):
        k0 = pl.multiple_of(kt * key_tile, key_tile)
        col_max = []
        for h in range(2):
            s = jnp.dot(ka_ref[0, 0, pl.ds(k0, key_tile), :], q_half[h], preferred_element_type=F32)
            s_ref[h] = s
            col_max.append(jnp.max(s, axis=0, keepdims=True))
        return jnp.concatenate(col_max, axis=1)

    def values(kt, p_ref, a):
        k0 = pl.multiple_of(kt * key_tile, key_tile)
        for h in range(2):
            lanes = slice(h * half, (h + 1) * half)
            acc_ref[:, lanes] = acc_ref[:, lanes] * a[:, lanes] + jnp.dot(
                vst_ref[0, :, pl.ds(k0, key_tile)], p_ref[h], preferred_element_type=F32)

    p1_ref[...] = jnp.zeros_like(p1_ref)
    cm_first = scores(0, s0_ref)

    d0 = pl.multiple_of(q0, Q_BLOCK)
    s = jnp.dot(ka_ref[0, 0, pl.ds(d0, Q_BLOCK), 0:Q_BLOCK], q_pad, preferred_element_type=F32)
    tq = lax.broadcasted_iota(jnp.int32, s.shape, 1) & (Q_BLOCK - 1)
    kl = lax.broadcasted_iota(jnp.int32, s.shape, 0)
    s = jnp.where(kl <= tq, s, NEG)
    m_own = jnp.max(s, axis=0, keepdims=True)
    p = jnp.exp2(s - m_own)
    acc_ref[...] = jnp.dot(vst_ref[0, :, pl.ds(d0, Q_BLOCK)], p.astype(BF16),
                           preferred_element_type=F32)

    wlen = WINDOW + Q_BLOCK
    w0 = pl.multiple_of(jnp.maximum(q0 - WINDOW, 0), Q_BLOCK)
    band = band_ref[pl.ds(pl.multiple_of(WINDOW - (q0 - w0), Q_BLOCK), wlen), :]
    sw = (jnp.dot(kw_ref[0, 0, pl.ds(w0, wlen), :], q_pad, preferred_element_type=F32)
          + jnp.concatenate([band] * NSA_REP, axis=1))
    e = jnp.exp2(sw - jnp.max(sw, axis=0, keepdims=True))
    ow_ref[...] = (jnp.dot(vwt_ref[0, :, pl.ds(w0, wlen)], e.astype(BF16), preferred_element_type=F32)
                   / jnp.sum(e, axis=0, keepdims=True))

    n_tiles = (q0 + key_tile - 1) // key_tile
    n_pairs = n_tiles // 2
    init = (m_own, jnp.sum(p, axis=0, keepdims=True), jnp.ones((1, LANES_Q), F32), cm_first)
    m, l, a_prev, cm_last = lax.fori_loop(0, n_pairs, pair, init)
    kt_prev = jnp.maximum(2 * n_pairs - 1, 0)

    def odd_tail(_):
        m_new = jnp.maximum(m, cm_last)
        a = jnp.exp2(m - m_new)
        sums = []
        for h in range(2):
            ph = jnp.exp2(s0_ref[h] - m_new[:, h * half:(h + 1) * half])
            p0_ref[h] = ph.astype(BF16)
            sums.append(jnp.sum(ph, axis=0, keepdims=True))
        values(kt_prev, p1_ref, a_prev)
        values(2 * n_pairs, p0_ref, a)
        return l * a + jnp.concatenate(sums, axis=1)

    def even_tail(_):
        values(kt_prev, p1_ref, a_prev)
        return l

    l = lax.cond(n_tiles % 2 == 1, odd_tail, even_tail, None)
    o_s = acc_ref[...] / l
    o_w = ow_ref[...]

    gates = _sigmoid(gt_ref[0, :, tok])
    outs = []
    for r in range(NSA_REP):
        sl = slice(r * Q_BLOCK, (r + 1) * Q_BLOCK)
        g_c, g_s, g_w = (gates[r * N_BRANCH + i:r * N_BRANCH + i + 1, :] for i in range(N_BRANCH))
        outs.append(g_c * oct_ref[0, r, :, tok] + g_s * o_s[:, sl] + g_w * o_w[:, sl])
    o_ref[0, tok, :] = jnp.concatenate(outs, axis=0).T.astype(BF16)


def _nsa_main(qt, nsel, k_aug, kw, vs_t, vw_t, oct_, gt, key_tile):
    bsz, _, t = qt.shape
    g = k_aug.shape[1]
    dh = NSA_HEAD_DIM
    ns = t // SLC_BLOCK
    mq = MAIN_BLOCKS * Q_BLOCK
    aw = k_aug.shape[-1]
    assert (t // key_tile) % 2 == 0 and key_tile % Q_BLOCK == 0 and t >= WINDOW + Q_BLOCK and t % mq == 0
    rel = np.arange(2 * WINDOW + Q_BLOCK)[:, None] - WINDOW - np.arange(Q_BLOCK)[None, :]
    band = jnp.asarray(np.where((rel <= 0) & (rel > -WINDOW), 0.0, NEG), dtype=F32)
    return pl.pallas_call(
        functools.partial(_nsa_main_kernel, key_tile=key_tile),
        out_shape=jax.ShapeDtypeStruct((bsz, t, NSA_WIDTH), BF16),
        grid=(bsz, g, t // mq),
        in_specs=[pl.BlockSpec((1, GROUP_Q, mq), lambda b, j, i: (b, j, i)),
                  pl.BlockSpec((1, 1, ns, mq), lambda b, j, i: (b, j, 0, i)),
                  pl.BlockSpec((1, 1, t, aw), lambda b, j, i: (b, j, 0, 0)),
                  pl.BlockSpec((1, dh, t), lambda b, j, i: (b, j, 0)),
                  pl.BlockSpec((1, 1, t, Q_BLOCK), lambda b, j, i: (b, j, 0, 0)),
                  pl.BlockSpec((1, dh, t), lambda b, j, i: (b, j, 0)),
                  pl.BlockSpec((1, NSA_REP, dh, mq), lambda b, j, i: (b, j, 0, i)),
                  pl.BlockSpec((1, GATE_ROWS, mq), lambda b, j, i: (b, j, i)),
                  _resident(band.shape)],
        out_specs=pl.BlockSpec((1, mq, GROUP_Q), lambda b, j, i: (b, i, j)),
        scratch_shapes=[pltpu.VMEM((MAIN_BLOCKS, 2, key_tile, LANES_Q // 2), F32),
                        pltpu.VMEM((MAIN_BLOCKS, 2, key_tile, LANES_Q // 2), F32),
                        pltpu.VMEM((MAIN_BLOCKS, 2, key_tile, LANES_Q // 2), BF16),
                        pltpu.VMEM((MAIN_BLOCKS, 2, key_tile, LANES_Q // 2), BF16),
                        pltpu.VMEM((MAIN_BLOCKS, dh, LANES_Q), F32), pltpu.VMEM((MAIN_BLOCKS, dh, LANES_Q), F32)],
        compiler_params=_params("parallel", "parallel", "arbitrary"),
        name="nsa_main",
    )(qt, nsel, k_aug, vs_t, kw, vw_t, oct_, gt, band)


def _retention_kernel(q_ref, k_ref, v_ref, gate_ref, cos_ref, sin_ref, gng_ref, gnb_ref,
                      o_ref, state_ref, decay_ref, xi_ref, zeta_ref):
    c = RET_CHUNK
    d = RET_HEAD_DIM

    @pl.when(pl.program_id(1) == 0)
    def _():
        state_ref[...] = jnp.zeros_like(state_ref)
        i_row = lax.broadcasted_iota(jnp.int32, (c, c), 0)
        rel = (i_row - lax.broadcasted_iota(jnp.int32, (c, c), 1)).astype(F32)
        pos = i_row.astype(F32)
        for h in range(RET_HEADS):
            log_g = math.log(1.0 - 2.0 ** (-5.0 - h))
            decay_ref[h] = jnp.where(rel >= 0, jnp.exp(jnp.maximum(rel, 0.0) * log_g), 0.0)
            xi_ref[h] = jnp.exp((pos + 1.0) * log_g)
            zeta_ref[h] = jnp.exp((c - 1.0 - pos) * log_g) * (d ** -0.5)

    for sub in range(q_ref.shape[1] // c):
        rows = slice(sub * c, (sub + 1) * c)
        cos = cos_ref[rows, :]
        sin = sin_ref[rows, :]
        for h in range(RET_HEADS):
            log_g = math.log(1.0 - 2.0 ** (-5.0 - h))
            sl = slice(h * d, (h + 1) * d)
            q = q_ref[0, rows, sl]
            k = k_ref[0, rows, sl]
            q = q * cos + pltpu.roll(q, d // 2, 1) * sin
            k = k * cos + pltpu.roll(k, d // 2, 1) * sin
            vb = v_ref[0, rows, sl].astype(BF16)
            inner = lax.dot_general(q.astype(BF16), (k * (d ** -0.5)).astype(BF16), (((1,), (1,)), ((), ())),
                                    preferred_element_type=F32) * decay_ref[h]
            o = jnp.dot(inner.astype(BF16), vb, preferred_element_type=F32)
            state = state_ref[h]
            o = o + jnp.dot((q * xi_ref[h]).astype(BF16), state.astype(BF16), preferred_element_type=F32)
            kv = jnp.dot((k * zeta_ref[h]).T.astype(BF16), vb, preferred_element_type=F32)
            state_ref[h] = math.exp(c * log_g) * state + kv
            mu = jnp.mean(o, axis=-1, keepdims=True)
            oc = o - mu
            var = jnp.mean(oc * oc, axis=-1, keepdims=True)
            y = oc * lax.rsqrt(var + LN_EPS) * gng_ref[:, sl] + gnb_ref[:, sl]
            gate = gate_ref[0, rows, sl]
            o_ref[0, rows, sl] = (gate * _sigmoid(gate) * y).astype(BF16)


def _rotary_tables(t):
    d = RET_HEAD_DIM
    ang = np.arange(t)[:, None] * ROPE_BASE ** (-np.arange(0, d, 2) / d)[None, :]
    cos, sin = np.cos(ang), np.sin(ang)
    return (jnp.asarray(np.concatenate([cos, cos], axis=1), dtype=F32),
            jnp.asarray(np.concatenate([-sin, sin], axis=1), dtype=F32))


def _retention(ret, gn_g, gn_b):
    bsz, t, _ = ret.shape
    c = RET_STEP
    cos2, sin2 = _rotary_tables(t)
    part = lambda j: pl.BlockSpec((1, c, RET_WIDTH), lambda b, n: (b, n, j))
    tab = pl.BlockSpec((c, RET_HEAD_DIM), lambda b, n: (n, 0))
    head_mat = pltpu.VMEM((RET_HEADS, RET_CHUNK, RET_HEAD_DIM), F32)
    return pl.pallas_call(
        _retention_kernel,
        out_shape=jax.ShapeDtypeStruct((bsz, t, RET_WIDTH), BF16),
        grid=(bsz, t // c),
        in_specs=[part(0), part(1), part(2), part(3), tab, tab,
                  _resident((1, RET_WIDTH)), _resident((1, RET_WIDTH))],
        out_specs=pl.BlockSpec((1, c, RET_WIDTH), lambda b, n: (b, n, 0)),
        scratch_shapes=[pltpu.VMEM((RET_HEADS, RET_HEAD_DIM, RET_HEAD_DIM), F32), head_mat, head_mat, head_mat],
        compiler_params=_params("parallel", "arbitrary"),
        name="retention",
    )(ret, ret, ret, ret, cos2, sin2, gn_g.reshape(1, -1), gn_b.reshape(1, -1))


def _tail_kernel(x_ref, on_ref, or_ref, p_ref, wn_ref, wr_ref, w13_ref, w2_ref, wp_ref, wg_ref,
                 g_ref, b_ref, o_ref):
    mix = (jnp.dot(on_ref[...], wn_ref[...], preferred_element_type=F32)
           + jnp.dot(or_ref[...], wr_ref[...], preferred_element_type=F32))
    x = _layer_norm(ALPHA * x_ref[...] + mix, g_ref[0:1, :], b_ref[0:1, :])
    x = _layer_norm(ALPHA * x + 0.5 * _swiglu(x.astype(BF16), w13_ref, w2_ref), g_ref[1:2, :], b_ref[1:2, :])
    e = (jnp.dot(p_ref[...].astype(BF16), wp_ref[...], preferred_element_type=F32)
         * _sigmoid(jnp.dot(x.astype(BF16), wg_ref[...], preferred_element_type=F32)))
    o_ref[...] = _layer_norm(ALPHA * x + e, g_ref[2:3, :], b_ref[2:3, :])


def _tail(x2d, o_nsa, o_ret, p2d, w_out, w13, w2, layer, w_ple, w_gate, g, b):
    n = x2d.shape[0]
    row = lambda w: pl.BlockSpec((ROW_TILE, w), lambda i: (i, 0))
    return pl.pallas_call(
        _tail_kernel,
        out_shape=jax.ShapeDtypeStruct((n, D_MODEL), F32),
        grid=(n // ROW_TILE,),
        in_specs=[row(D_MODEL), row(NSA_WIDTH), row(RET_WIDTH), row(PLE_DIM),
                  _resident((NSA_WIDTH, D_MODEL)), _resident((RET_WIDTH, D_MODEL)),
                  *_ffn_weight_specs(layer, 1),
                  _resident((PLE_DIM, D_MODEL)), _resident((D_MODEL, D_MODEL)),
                  _resident((3, D_MODEL)), _resident((3, D_MODEL))],
        out_specs=row(D_MODEL),
        compiler_params=_params("parallel"),
        name="tail",
    )(x2d, o_nsa, o_ret, p2d, w_out[:NSA_WIDTH].astype(BF16), w_out[NSA_WIDTH:].astype(BF16),
      w13, w2, w_ple.astype(BF16), w_gate.astype(BF16), g, b)


def _token_mixer(x, w_in, cmp_pos, cmp_w1, cmp_w2, gn_g, gn_b):
    t = x.shape[1]
    raw_k, raw_v, ret, k_aug, kw, qt, vs_t, vw_t, gt = _in_proj(x, w_in)
    kc, vct = _compress(raw_k, raw_v, cmp_pos, cmp_w1, cmp_w2)
    oct_, nsel = _nsa_select(qt, kc, vct)
    o_nsa = _nsa_main(qt, nsel, k_aug, kw, vs_t, vw_t, oct_, gt, min(512, t))
    o_ret = _retention(ret, gn_g, gn_b)
    return o_nsa, o_ret


def kernel(x, p, ffn_w13, ffn_w2, w_in, cmp_pos, cmp_w1, cmp_w2, ret_gn_g, ret_gn_b,
           w_out, w_ple, w_ple_gate, ln_g, ln_b):
    bsz, t, d = x.shape
    n = bsz * t
    w13, w2 = ffn_w13.astype(BF16), ffn_w2.astype(BF16)
    for i in range(ffn_w13.shape[0]):
        h = _ffn_ln(x.reshape(n, d), w13, w2, i, 0, ln_g[i, 0], ln_b[i, 0])
        o_nsa, o_ret = _token_mixer(h.reshape(bsz, t, d), w_in[i], cmp_pos[i], cmp_w1[i], cmp_w2[i],
                                    ret_gn_g[i], ret_gn_b[i])
        h = _tail(h, o_nsa.reshape(n, -1), o_ret.reshape(n, -1), p[i].reshape(n, -1), w_out[i],
                  w13, w2, i, w_ple[i], w_ple_gate[i], ln_g[i, 1:], ln_b[i, 1:])
        x = h.reshape(bsz, t, d)
    return x
```

```python
import functools
import math

import jax
import jax.numpy as jnp
import numpy as np
from jax import lax
from jax.experimental import pallas as pl
from jax.experimental.pallas import tpu as pltpu

F32 = jnp.float32
BF16 = jnp.bfloat16

D_MODEL = 1024
D_FF = 2816
NSA_GROUPS = 2
NSA_REP = 4
NSA_HEAD_DIM = 64
N_BRANCH = 3
CMP_BLOCK = 32
CMP_STRIDE = 16
CMP_HIDDEN = 256
SLC_BLOCK = 64
SLC_TOPK = 16
WINDOW = 512
Q_BLOCK = 128
RET_HEADS = 4
RET_HEAD_DIM = 128
RET_CHUNK = 128
RET_STEP = 512
ROPE_BASE = 10000.0
PLE_DIM = 256
DEPTH = 1
ALPHA = (2.0 * DEPTH) ** 0.25
LN_EPS = 1e-5
NEG = -1e30
LOG2_E = math.log2(math.e)

NSA_WIDTH = NSA_GROUPS * NSA_REP * NSA_HEAD_DIM
RET_WIDTH = RET_HEADS * RET_HEAD_DIM
GROUP_Q = NSA_REP * NSA_HEAD_DIM
LANES_Q = NSA_REP * Q_BLOCK
GATE_ROWS = 16
KV_WIDTH = NSA_GROUPS * NSA_HEAD_DIM
TR_Q, TR_VS, TR_VW, TR_G = 0, 512, 640, 768
TR_ROWS = TR_G + NSA_GROUPS * GATE_ROWS
NAT_CMP, NAT_RET = 0, 2 * KV_WIDTH
NAT_KS = NAT_RET + 4 * RET_WIDTH
NAT_KW = NAT_KS + NSA_GROUPS * Q_BLOCK
NAT_COLS = NAT_KW + NSA_GROUPS * Q_BLOCK

KEY_CHUNKS = 1
SUM_ROWS = 16
SEL_BLOCKS = 8
SEL_VARIANTS = 4
MAIN_BLOCKS = 1
VMEM_LIMIT = 56 * 1024 * 1024
ROW_TILE = 512


def _layer_norm(z, g, b):
    mu = jnp.mean(z, axis=-1, keepdims=True)
    zc = z - mu
    var = jnp.mean(zc * zc, axis=-1, keepdims=True)
    return zc * lax.rsqrt(var + LN_EPS) * g + b


def _sigmoid(v):
    return 1.0 / (1.0 + jnp.exp(-v))


def _params(*sem):
    return pltpu.CompilerParams(dimension_semantics=sem, vmem_limit_bytes=VMEM_LIMIT)


def _resident(shape):
    nd = len(shape)
    return pl.BlockSpec(shape, lambda *_: (0,) * nd, pipeline_mode=pl.Buffered(1))


def _swiglu(xb, w13_ref, w2_ref):
    a = jnp.dot(xb, w13_ref[0, 0, :, :D_FF], preferred_element_type=F32)
    u = jnp.dot(xb, w13_ref[0, 0, :, D_FF:], preferred_element_type=F32)
    h = (a * _sigmoid(a) * u).astype(BF16)
    return jnp.dot(h, w2_ref[0, 0], preferred_element_type=F32)


def _ffn_weight_specs(layer, which):
    pick = lambda *_: (layer, which, 0, 0)
    return [pl.BlockSpec((1, 1, D_MODEL, 2 * D_FF), pick, pipeline_mode=pl.Buffered(1)),
            pl.BlockSpec((1, 1, D_FF, D_MODEL), pick, pipeline_mode=pl.Buffered(1))]


def _ffn_ln_kernel(x_ref, w13_ref, w2_ref, g_ref, b_ref, o_ref):
    x = x_ref[...]
    o_ref[...] = _layer_norm(ALPHA * x + 0.5 * _swiglu(x.astype(BF16), w13_ref, w2_ref), g_ref[...], b_ref[...])


def _ffn_ln(x2d, w13, w2, layer, which, g, b):
    n = x2d.shape[0]
    row = pl.BlockSpec((ROW_TILE, D_MODEL), lambda i: (i, 0))
    return pl.pallas_call(
        _ffn_ln_kernel,
        out_shape=jax.ShapeDtypeStruct((n, D_MODEL), F32),
        grid=(n // ROW_TILE,),
        in_specs=[row, *_ffn_weight_specs(layer, which), _resident((1, D_MODEL)), _resident((1, D_MODEL))],
        out_specs=row,
        compiler_params=_params("parallel"),
        name="ffn_ln",
    )(x2d, w13, w2, g.reshape(1, -1), b.reshape(1, -1))


def _in_proj_kernel(x_ref, wn_ref, wt_ref, ck_ref, cv_ref, ret_ref, ka_ref, kw_ref, qt_ref, vst_ref, vwt_ref,
                    gt_ref):
    xb = x_ref[0].astype(BF16)
    tm = xb.shape[0]
    nat = jnp.dot(xb, wn_ref[...], preferred_element_type=F32)
    ck_ref[0] = nat[:, NAT_CMP:NAT_CMP + KV_WIDTH]
    cv_ref[0] = nat[:, NAT_CMP + KV_WIDTH:NAT_RET]
    ret_ref[0] = nat[:, NAT_RET:NAT_KS]
    ns = ka_ref.shape[3] - Q_BLOCK
    row = pl.program_id(1) * tm + lax.broadcasted_iota(jnp.int32, (tm, ns), 0)
    own = row // SLC_BLOCK == lax.broadcasted_iota(jnp.int32, (tm, ns), 1)
    bias = jnp.where(own, NEG, 0.0).astype(BF16)
    for g in range(NSA_GROUPS):
        ka_ref[0, g, :, 0:Q_BLOCK] = nat[:, NAT_KS + g * Q_BLOCK:NAT_KS + (g + 1) * Q_BLOCK].astype(BF16)
        ka_ref[0, g, :, Q_BLOCK:] = bias
        kw_ref[0, g] = nat[:, NAT_KW + g * Q_BLOCK:NAT_KW + (g + 1) * Q_BLOCK].astype(BF16)
    tr = lax.dot_general(wt_ref[...], xb, (((1,), (1,)), ((), ())), preferred_element_type=F32)
    qt_ref[0] = (tr[TR_Q:TR_VS] * (NSA_HEAD_DIM ** -0.5 * LOG2_E)).astype(BF16)
    vst_ref[0] = tr[TR_VS:TR_VW].astype(BF16)
    vwt_ref[0] = tr[TR_VW:TR_G].astype(BF16)
    gt_ref[0] = tr[TR_G:]


def _split_w_in(w_in):
    c_q = NSA_WIDTH
    c_kv = c_q + 2 * N_BRANCH * KV_WIDTH
    c_g = c_kv + NSA_GROUPS * NSA_REP * N_BRANCH
    kv = w_in[:, c_q:c_kv].reshape(D_MODEL, 2 * N_BRANCH, KV_WIDTH)
    gates = w_in[:, c_kv:c_g].reshape(D_MODEL, NSA_GROUPS, NSA_REP * N_BRANCH)
    gates = jnp.pad(gates, ((0, 0), (0, 0), (0, GATE_ROWS - NSA_REP * N_BRANCH)))
    pad_groups = lambda w: jnp.pad(w.reshape(D_MODEL, NSA_GROUPS, NSA_HEAD_DIM),
                                   ((0, 0), (0, 0), (0, Q_BLOCK - NSA_HEAD_DIM))).reshape(D_MODEL, -1)
    w_nat = jnp.concatenate([kv[:, 0], kv[:, 1], w_in[:, c_g:], pad_groups(kv[:, 2]), pad_groups(kv[:, 4])],
                            axis=1)
    w_tr = jnp.concatenate([w_in[:, :c_q], kv[:, 3], kv[:, 5],
                            gates.reshape(D_MODEL, NSA_GROUPS * GATE_ROWS)], axis=1)
    return w_nat.astype(BF16), w_tr.T.astype(BF16)


def _in_proj(x, w_in):
    bsz, t, _ = x.shape
    g = NSA_GROUPS
    ns = t // SLC_BLOCK
    w_nat, w_trt = _split_w_in(w_in)
    rows = lambda w: pl.BlockSpec((1, ROW_TILE, w), lambda b, i: (b, i, 0))
    cols = lambda h: pl.BlockSpec((1, h, ROW_TILE), lambda b, i: (b, 0, i))
    grouped = lambda w: pl.BlockSpec((1, g, ROW_TILE, w), lambda b, i: (b, 0, i, 0))
    return pl.pallas_call(
        _in_proj_kernel,
        out_shape=(jax.ShapeDtypeStruct((bsz, t, KV_WIDTH), F32),
                   jax.ShapeDtypeStruct((bsz, t, KV_WIDTH), F32),
                   jax.ShapeDtypeStruct((bsz, t, NAT_KS - NAT_RET), F32),
                   jax.ShapeDtypeStruct((bsz, g, t, Q_BLOCK + ns), BF16),
                   jax.ShapeDtypeStruct((bsz, g, t, Q_BLOCK), BF16),
                   jax.ShapeDtypeStruct((bsz, TR_VS - TR_Q, t), BF16),
                   jax.ShapeDtypeStruct((bsz, TR_VW - TR_VS, t), BF16),
                   jax.ShapeDtypeStruct((bsz, TR_G - TR_VW, t), BF16),
                   jax.ShapeDtypeStruct((bsz, TR_ROWS - TR_G, t), F32)),
        grid=(bsz, t // ROW_TILE),
        in_specs=[rows(D_MODEL), _resident((D_MODEL, NAT_COLS)), _resident((TR_ROWS, D_MODEL))],
        out_specs=(rows(KV_WIDTH), rows(KV_WIDTH), rows(NAT_KS - NAT_RET), grouped(Q_BLOCK + ns), grouped(Q_BLOCK),
                   cols(TR_VS - TR_Q), cols(TR_VW - TR_VS), cols(TR_G - TR_VW), cols(TR_ROWS - TR_G)),
        compiler_params=_params("parallel", "parallel"),
        name="in_proj",
    )(x, w_nat, w_trt)


def _gelu_tanh(v):
    return 0.5 * v * (1.0 + jnp.tanh(math.sqrt(2.0 / math.pi) * (v + 0.044715 * (v * v * v))))


def _compress_hidden(h, pos_ref, w1_ref):
    top = jnp.dot((h + pos_ref[0:1, :]).astype(BF16), w1_ref[0], preferred_element_type=F32)
    bot = jnp.dot((h + pos_ref[1:2, :]).astype(BF16), w1_ref[1], preferred_element_type=F32)
    nc = h.shape[0]
    return _gelu_tanh(top + pltpu.roll(bot, nc - 1, 0)).astype(BF16)


def _compress_kernel(rk_ref, rv_ref, pk_ref, pv_ref, w1k_ref, w1v_ref, w2k_ref, w2vt_ref, kc_ref, vct_ref):
    dh = NSA_HEAD_DIM
    nc = kc_ref.shape[2]

    def half_blocks(raw_ref, g):
        rows = [raw_ref[0, pl.ds(l, nc, stride=CMP_STRIDE), :] for l in range(CMP_STRIDE)]
        return jnp.concatenate([r[:, g * dh:(g + 1) * dh] for r in rows], axis=1)

    row = lax.broadcasted_iota(jnp.int32, (nc, dh), 0)
    col = lax.broadcasted_iota(jnp.int32, (dh, nc), 1)
    for g in range(NSA_GROUPS):
        gk = _compress_hidden(half_blocks(rk_ref, g), pk_ref, w1k_ref)
        kc = jnp.dot(gk, w2k_ref[...], preferred_element_type=F32)
        kc_ref[0, g] = jnp.where(row < nc - 1, kc, 0.0).astype(BF16)
        gv = _compress_hidden(half_blocks(rv_ref, g), pv_ref, w1v_ref)
        vct = lax.dot_general(w2vt_ref[...], gv, (((1,), (1,)), ((), ())),
                              preferred_element_type=F32)
        vct_ref[0, g, 0:dh, :] = jnp.where(col < nc - 1, vct, 0.0).astype(BF16)
        vct_ref[0, g, dh:, :] = jnp.ones((SUM_ROWS, nc), BF16)


def _compress(raw_k, raw_v, cmp_pos, cmp_w1, cmp_w2):
    bsz, t, width = raw_k.shape
    g, dh = NSA_GROUPS, NSA_HEAD_DIM
    nc = t // CMP_STRIDE
    half = CMP_BLOCK // 2
    hw = half * dh
    pos = cmp_pos.reshape(2, 2, hw)
    w1 = cmp_w1.reshape(2, 2, hw, CMP_HIDDEN).astype(BF16)
    return pl.pallas_call(
        _compress_kernel,
        out_shape=(jax.ShapeDtypeStruct((bsz, g, nc, dh), BF16),
                   jax.ShapeDtypeStruct((bsz, g, dh + SUM_ROWS, nc), BF16)),
        grid=(bsz,),
        in_specs=[pl.BlockSpec((1, t, width), lambda b: (b, 0, 0)), pl.BlockSpec((1, t, width), lambda b: (b, 0, 0)),
                  _resident((2, hw)), _resident((2, hw)),
                  _resident((2, hw, CMP_HIDDEN)), _resident((2, hw, CMP_HIDDEN)),
                  _resident((CMP_HIDDEN, dh)), _resident((dh, CMP_HIDDEN))],
        out_specs=(pl.BlockSpec((1, g, nc, dh), lambda b: (b, 0, 0, 0)),
                   pl.BlockSpec((1, g, dh + SUM_ROWS, nc), lambda b: (b, 0, 0, 0))),
        compiler_params=_params("parallel"),
        name="compress",
    )(raw_k, raw_v, pos[0], pos[1], w1[0], w1[1], cmp_w2[0].astype(BF16), cmp_w2[1].T.astype(BF16))


def _q_lanes(qt_ref, j=0):
    qt = qt_ref[0, :, j * Q_BLOCK:(j + 1) * Q_BLOCK]
    return jnp.concatenate([qt[r * NSA_HEAD_DIM:(r + 1) * NSA_HEAD_DIM, :] for r in range(NSA_REP)], axis=1)


def _lane_token(shape, q0):
    return q0 + (lax.broadcasted_iota(jnp.int32, shape, 1) & (Q_BLOCK - 1))


def _nsa_select_kernel(qt_ref, kc_ref, vct_ref, ovl_ref, bias_ref, oct_ref, nsel_ref, imp_ref):
    step = pl.program_id(2)
    nc = kc_ref.shape[2]
    last_block = ((step + 1) * SEL_BLOCKS * Q_BLOCK - CMP_BLOCK) // CMP_STRIDE
    quarter = nc // SEL_VARIANTS
    branches = [functools.partial(_nsa_select_rows, (v + 1) * quarter, qt_ref, kc_ref, vct_ref, ovl_ref, bias_ref,
                                  oct_ref, nsel_ref, imp_ref) for v in range(SEL_VARIANTS)]
    lax.switch(jnp.minimum(last_block // quarter, SEL_VARIANTS - 1), branches)


def _nsa_select_rows(nc, qt_ref, kc_ref, vct_ref, ovl_ref, bias_ref, oct_ref, nsel_ref, imp_ref):
    step = pl.program_id(2)
    dh = NSA_HEAD_DIM
    nc_all = kc_ref.shape[2]
    kc = kc_ref[0, 0, 0:nc, :]
    lhs = jnp.concatenate([ovl_ref[:, 0:nc], vct_ref[0, 0, :, 0:nc]], axis=0)
    n_blk = ovl_ref.shape[0]

    def scores(j):
        qb = step * SEL_BLOCKS + j
        s = jnp.dot(kc, _q_lanes(qt_ref, j), preferred_element_type=F32)
        bias = bias_ref[pl.ds(pl.multiple_of(nc_all - qb * (Q_BLOCK // CMP_STRIDE), 8), nc), :]
        return s + jnp.concatenate([bias] * NSA_REP, axis=1)

    def select(pair):
        lanes = slice(pair * 2 * Q_BLOCK, (pair + 1) * 2 * Q_BLOCK)
        imp = imp_ref[:, lanes]
        blk = lax.broadcasted_iota(jnp.int32, imp.shape, 0)
        t = ((step * SEL_BLOCKS + 2 * pair) * Q_BLOCK
             + lax.broadcasted_iota(jnp.int32, imp.shape, 1))
        cur = t // SLC_BLOCK
        valid = blk <= cur
        forced = (blk == 0) | (blk >= cur - 1)
        val0 = jnp.where(valid & jnp.logical_not(forced), imp, -jnp.inf)
        blk_f = blk.astype(F32)
        val = val0
        for _ in range(max(min(SLC_TOPK, n_blk) - 3, 0)):
            top = jnp.max(val, axis=0, keepdims=True)
            first = jnp.min(jnp.where(val == top, blk_f, float(n_blk)), axis=0, keepdims=True)
            val = jnp.where(blk_f == first, -jnp.inf, val)
            yield
        chosen = valid & (forced | (val != val0))
        nsel_ref[0, 0, :, lanes] = jnp.where(chosen & (blk < 2 * (t // Q_BLOCK)), 0.0, 1.0).astype(BF16)

    running = []

    def advance(n):
        for _ in range(n):
            if running and next(running[0], True):
                running.pop(0)

    s_next = scores(0)
    for j in range(SEL_BLOCKS):
        qb = step * SEL_BLOCKS + j
        s = s_next
        if j + 1 < SEL_BLOCKS:
            s_next = scores(j + 1)
        advance(3)
        e = jnp.exp2(s - jnp.max(s, axis=0, keepdims=True)).astype(BF16)
        advance(3)
        r = jnp.dot(lhs, e, preferred_element_type=F32)
        tok = _lane_token((1, LANES_Q), qb * Q_BLOCK)
        scale = jnp.where(tok >= CMP_BLOCK - 1, 1.0 / r[n_blk + dh:n_blk + dh + 1], 0.0)
        o = r[n_blk:n_blk + dh] * scale
        w = r[0:n_blk] * scale
        imp = w[:, 0:Q_BLOCK]
        for h in range(NSA_REP):
            oct_ref[0, h, :, j * Q_BLOCK:(j + 1) * Q_BLOCK] = o[:, h * Q_BLOCK:(h + 1) * Q_BLOCK]
            if h:
                imp = imp + w[:, h * Q_BLOCK:(h + 1) * Q_BLOCK]
        imp_ref[:, j * Q_BLOCK:(j + 1) * Q_BLOCK] = imp
        advance(3)
        if j % 2 == 1:
            running.append(select(j // 2))
    while running:
        advance(1)


def _cmp_bias_table(t):
    nc = t // CMP_STRIDE
    rel = (np.arange(2 * nc)[:, None] - nc) * CMP_STRIDE + CMP_BLOCK - 1
    return jnp.asarray(np.where(rel <= np.arange(Q_BLOCK)[None, :], 0.0, NEG), dtype=F32)


def _overlap_t(t):
    n_cmp = t // CMP_STRIDE
    n_slc = t // SLC_BLOCK
    cs = np.arange(n_cmp) * CMP_STRIDE
    ss = np.arange(n_slc) * SLC_BLOCK
    ov = np.clip(np.minimum(cs[None, :] + CMP_BLOCK, ss[:, None] + SLC_BLOCK)
                 - np.maximum(cs[None, :], ss[:, None]), 0, None) / CMP_BLOCK
    return jnp.asarray(ov, dtype=BF16)


def _nsa_select(qt, kc, vct):
    bsz, _, t = qt.shape
    g, nc, dh = kc.shape[1:]
    ns = t // SLC_BLOCK
    sq = SEL_BLOCKS * Q_BLOCK
    assert ns % 16 == 0 and t % sq == 0 and SEL_BLOCKS % 2 == 0
    return pl.pallas_call(
        _nsa_select_kernel,
        out_shape=(jax.ShapeDtypeStruct((bsz, g * NSA_REP, dh, t), F32),
                   jax.ShapeDtypeStruct((bsz, g, ns, t), BF16)),
        grid=(bsz, g, t // sq),
        in_specs=[pl.BlockSpec((1, GROUP_Q, sq), lambda b, j, i: (b, j, i)),
                  pl.BlockSpec((1, 1, nc, dh), lambda b, j, i: (b, j, 0, 0)),
                  pl.BlockSpec((1, 1, dh + SUM_ROWS, nc), lambda b, j, i: (b, j, 0, 0)),
                  _resident((ns, nc)), _resident((2 * nc, Q_BLOCK))],
        out_specs=(pl.BlockSpec((1, NSA_REP, dh, sq), lambda b, j, i: (b, j, 0, i)),
                   pl.BlockSpec((1, 1, ns, sq), lambda b, j, i: (b, j, 0, i))),
        scratch_shapes=[pltpu.VMEM((ns, sq), F32)],
        compiler_params=_params("parallel", "parallel", "parallel"),
        name="nsa_select",
    )(qt, kc, vct, _overlap_t(t), _cmp_bias_table(t))


def _nsa_main_kernel(qt_ref, nsel_ref, ka_ref, vst_ref, kw_ref, vwt_ref, oct_ref, gt_ref, band_ref,
                     o_ref, s0_ref, s1_ref, p0_ref, p1_ref, acc_ref, ow_ref, *, key_tile):
    for j in range(MAIN_BLOCKS):
        _nsa_main_block(j, qt_ref, nsel_ref, ka_ref, vst_ref, kw_ref, vwt_ref, oct_ref, gt_ref, band_ref,
                        o_ref, s0_ref.at[j], s1_ref.at[j], p0_ref.at[j], p1_ref.at[j], acc_ref.at[j],
                        ow_ref.at[j], key_tile)


def _nsa_main_block(j, qt_ref, nsel_ref, ka_ref, vst_ref, kw_ref, vwt_ref, oct_ref, gt_ref, band_ref,
                    o_ref, s0_ref, s1_ref, p0_ref, p1_ref, acc_ref, ow_ref, key_tile):
    tok = slice(j * Q_BLOCK, (j + 1) * Q_BLOCK)
    qb = pl.program_id(2) * MAIN_BLOCKS + j
    q0 = qb * Q_BLOCK
    dh = NSA_HEAD_DIM
    last_tile = ka_ref.shape[2] // key_tile - 1
    qm = _q_lanes(qt_ref, j)
    nsel = nsel_ref[0, 0, :, tok]
    q_pad = jnp.concatenate([qm, jnp.zeros((Q_BLOCK - dh, LANES_Q), BF16)], axis=0)
    q_aug = jnp.concatenate([q_pad, jnp.concatenate([nsel] * NSA_REP, axis=1)], axis=0)

    half = LANES_Q // 2
    chunk = key_tile // KEY_CHUNKS
    q_half = [q_aug[:, h * half:(h + 1) * half] for h in range(2)]
    pieces = [(h, c) for c in range(KEY_CHUNKS) for h in range(2)]

    def step(s_in, p_out, col_max, m, l, kt_v, p_in, a_v, kt_n, s_out):
        m_new = jnp.maximum(m, col_max)
        a = jnp.exp2(m - m_new)
        kv0 = pl.multiple_of(kt_v * key_tile, key_tile)
        kn0 = pl.multiple_of(kt_n * key_tile, key_tile)
        sums = [None, None]
        new_max = [None, None]
        acc = [acc_ref[:, h * half:(h + 1) * half] * a_v[:, h * half:(h + 1) * half] for h in range(2)]
        for h, c in pieces:
            lanes = slice(h * half, (h + 1) * half)
            rows = slice(c * chunk, (c + 1) * chunk)
            p = jnp.exp2(s_in[h, rows, :] - m_new[:, lanes])
            p_out[h, rows, :] = p.astype(BF16)
            ps = jnp.sum(p, axis=0, keepdims=True)
            sums[h] = ps if sums[h] is None else sums[h] + ps
            acc[h] = acc[h] + jnp.dot(vst_ref[0, :, pl.ds(kv0 + c * chunk, chunk)], p_in[h, rows, :],
                                      preferred_element_type=F32)
            s = jnp.dot(ka_ref[0, 0, pl.ds(kn0 + c * chunk, chunk), :], q_half[h],
                        preferred_element_type=F32)
            s_out[h, rows, :] = s
            cm = jnp.max(s, axis=0, keepdims=True)
            new_max[h] = cm if new_max[h] is None else jnp.maximum(new_max[h], cm)
        for h in range(2):
            acc_ref[:, h * half:(h + 1) * half] = acc[h]
        return (m_new, l * a + jnp.concatenate(sums, axis=1), a, jnp.concatenate(new_max, axis=1))

    def pair(j, carry):
        m, l, a_prev, cm0 = carry
        t0 = 2 * j
        m, l, a0, cm1 = step(s0_ref, p0_ref, cm0, m, l,
                             jnp.maximum(t0 - 1, 0), p1_ref, a_prev, t0 + 1, s1_ref)
        m, l, a1, cm0 = step(s1_ref, p1_ref, cm1, m, l,
                             t0, p0_ref, a0, jnp.minimum(t0 + 2, last_tile), s0_ref)
        return m, l, a1, cm0

    def scores(kt, s_ref):
        k0 = pl.multiple_of(kt * key_tile, key_tile)
        col_max = []
        for h in range(2):
            s = jnp.dot(ka_ref[0, 0, pl.ds(k0, key_tile), :], q_half[h], preferred_element_type=F32)
            s_ref[h] = s
            col_max.append(jnp.max(s, axis=0, keepdims=True))
        return jnp.concatenate(col_max, axis=1)

    def values(kt, p_ref, a):
        k0 = pl.multiple_of(kt * key_tile, key_tile)
        for h in range(2):
            lanes = slice(h * half, (h + 1) * half)
            acc_ref[:, lanes] = acc_ref[:, lanes] * a[:, lanes] + jnp.dot(
                vst_ref[0, :, pl.ds(k0, key_tile)], p_ref[h], preferred_element_type=F32)

    p1_ref[...] = jnp.zeros_like(p1_ref)
    cm_first = scores(0, s0_ref)

    d0 = pl.multiple_of(q0, Q_BLOCK)
    s = jnp.dot(ka_ref[0, 0, pl.ds(d0, Q_BLOCK), 0:Q_BLOCK], q_pad, preferred_element_type=F32)
    tq = lax.broadcasted_iota(jnp.int32, s.shape, 1) & (Q_BLOCK - 1)
    kl = lax.broadcasted_iota(jnp.int32, s.shape, 0)
    s = jnp.where(kl <= tq, s, NEG)
    m_own = jnp.max(s, axis=0, keepdims=True)
    p = jnp.exp2(s - m_own)
    acc_ref[...] = jnp.dot(vst_ref[0, :, pl.ds(d0, Q_BLOCK)], p.astype(BF16),
                           preferred_element_type=F32)

    wlen = WINDOW + Q_BLOCK
    w0 = pl.multiple_of(jnp.maximum(q0 - WINDOW, 0), Q_BLOCK)
    band = band_ref[pl.ds(pl.multiple_of(WINDOW - (q0 - w0), Q_BLOCK), wlen), :]
    sw = (jnp.dot(kw_ref[0, 0, pl.ds(w0, wlen), :], q_pad, preferred_element_type=F32)
          + jnp.concatenate([band] * NSA_REP, axis=1))
    e = jnp.exp2(sw - jnp.max(sw, axis=0, keepdims=True))
    ow_ref[...] = (jnp.dot(vwt_ref[0, :, pl.ds(w0, wlen)], e.astype(BF16), preferred_element_type=F32)
                   / jnp.sum(e, axis=0, keepdims=True))

    n_tiles = (q0 + key_tile - 1) // key_tile
    n_pairs = n_tiles // 2
    init = (m_own, jnp.sum(p, axis=0, keepdims=True), jnp.ones((1, LANES_Q), F32), cm_first)
    m, l, a_prev, cm_last = lax.fori_loop(0, n_pairs, pair, init)
    kt_prev = jnp.maximum(2 * n_pairs - 1, 0)

    def odd_tail(_):
        m_new = jnp.maximum(m, cm_last)
        a = jnp.exp2(m - m_new)
        sums = []
        for h in range(2):
            ph = jnp.exp2(s0_ref[h] - m_new[:, h * half:(h + 1) * half])
            p0_ref[h] = ph.astype(BF16)
            sums.append(jnp.sum(ph, axis=0, keepdims=True))
        values(kt_prev, p1_ref, a_prev)
        values(2 * n_pairs, p0_ref, a)
        return l * a + jnp.concatenate(sums, axis=1)

    def even_tail(_):
        values(kt_prev, p1_ref, a_prev)
        return l

    l = lax.cond(n_tiles % 2 == 1, odd_tail, even_tail, None)
    o_s = acc_ref[...] / l
    o_w = ow_ref[...]

    gates = _sigmoid(gt_ref[0, :, tok])
    outs = []
    for r in range(NSA_REP):
        sl = slice(r * Q_BLOCK, (r + 1) * Q_BLOCK)
        g_c, g_s, g_w = (gates[r * N_BRANCH + i:r * N_BRANCH + i + 1, :] for i in range(N_BRANCH))
        outs.append(g_c * oct_ref[0, r, :, tok] + g_s * o_s[:, sl] + g_w * o_w[:, sl])
    o_ref[0, tok, :] = jnp.concatenate(outs, axis=0).T.astype(BF16)


def _nsa_main(qt, nsel, k_aug, kw, vs_t, vw_t, oct_, gt, key_tile):
    bsz, _, t = qt.shape
    g = k_aug.shape[1]
    dh = NSA_HEAD_DIM
    ns = t // SLC_BLOCK
    mq = MAIN_BLOCKS * Q_BLOCK
    aw = k_aug.shape[-1]
    assert (t // key_tile) % 2 == 0 and key_tile % Q_BLOCK == 0 and t >= WINDOW + Q_BLOCK and t % mq == 0
    rel = np.arange(2 * WINDOW + Q_BLOCK)[:, None] - WINDOW - np.arange(Q_BLOCK)[None, :]
    band = jnp.asarray(np.where((rel <= 0) & (rel > -WINDOW), 0.0, NEG), dtype=F32)
    return pl.pallas_call(
        functools.partial(_nsa_main_kernel, key_tile=key_tile),
        out_shape=jax.ShapeDtypeStruct((bsz, t, NSA_WIDTH), BF16),
        grid=(bsz, g, t // mq),
        in_specs=[pl.BlockSpec((1, GROUP_Q, mq), lambda b, j, i: (b, j, i)),
                  pl.BlockSpec((1, 1, ns, mq), lambda b, j, i: (b, j, 0, i)),
                  pl.BlockSpec((1, 1, t, aw), lambda b, j, i: (b, j, 0, 0)),
                  pl.BlockSpec((1, dh, t), lambda b, j, i: (b, j, 0)),
                  pl.BlockSpec((1, 1, t, Q_BLOCK), lambda b, j, i: (b, j, 0, 0)),
                  pl.BlockSpec((1, dh, t), lambda b, j, i: (b, j, 0)),
                  pl.BlockSpec((1, NSA_REP, dh, mq), lambda b, j, i: (b, j, 0, i)),
                  pl.BlockSpec((1, GATE_ROWS, mq), lambda b, j, i: (b, j, i)),
                  _resident(band.shape)],
        out_specs=pl.BlockSpec((1, mq, GROUP_Q), lambda b, j, i: (b, i, j)),
        scratch_shapes=[pltpu.VMEM((MAIN_BLOCKS, 2, key_tile, LANES_Q // 2), F32),
                        pltpu.VMEM((MAIN_BLOCKS, 2, key_tile, LANES_Q // 2), F32),
                        pltpu.VMEM((MAIN_BLOCKS, 2, key_tile, LANES_Q // 2), BF16),
                        pltpu.VMEM((MAIN_BLOCKS, 2, key_tile, LANES_Q // 2), BF16),
                        pltpu.VMEM((MAIN_BLOCKS, dh, LANES_Q), F32), pltpu.VMEM((MAIN_BLOCKS, dh, LANES_Q), F32)],
        compiler_params=_params("parallel", "parallel", "arbitrary"),
        name="nsa_main",
    )(qt, nsel, k_aug, vs_t, kw, vw_t, oct_, gt, band)


def _retention_kernel(q_ref, k_ref, v_ref, gate_ref, cos_ref, sin_ref, gng_ref, gnb_ref,
                      o_ref, state_ref, decay_ref, xi_ref, zeta_ref):
    c = RET_CHUNK
    d = RET_HEAD_DIM

    @pl.when(pl.program_id(1) == 0)
    def _():
        state_ref[...] = jnp.zeros_like(state_ref)
        i_row = lax.broadcasted_iota(jnp.int32, (c, c), 0)
        rel = (i_row - lax.broadcasted_iota(jnp.int32, (c, c), 1)).astype(F32)
        pos = i_row.astype(F32)
        for h in range(RET_HEADS):
            log_g = math.log(1.0 - 2.0 ** (-5.0 - h))
            decay_ref[h] = jnp.where(rel >= 0, jnp.exp(jnp.maximum(rel, 0.0) * log_g), 0.0)
            xi_ref[h] = jnp.exp((pos + 1.0) * log_g)
            zeta_ref[h] = jnp.exp((c - 1.0 - pos) * log_g) * (d ** -0.5)

    for sub in range(q_ref.shape[1] // c):
        rows = slice(sub * c, (sub + 1) * c)
        cos = cos_ref[rows, :]
        sin = sin_ref[rows, :]
        for h in range(RET_HEADS):
            log_g = math.log(1.0 - 2.0 ** (-5.0 - h))
            sl = slice(h * d, (h + 1) * d)
            q = q_ref[0, rows, sl]
            k = k_ref[0, rows, sl]
            q = q * cos + pltpu.roll(q, d // 2, 1) * sin
            k = k * cos + pltpu.roll(k, d // 2, 1) * sin
            vb = v_ref[0, rows, sl].astype(BF16)
            inner = lax.dot_general(q.astype(BF16), (k * (d ** -0.5)).astype(BF16), (((1,), (1,)), ((), ())),
                                    preferred_element_type=F32) * decay_ref[h]
            o = jnp.dot(inner.astype(BF16), vb, preferred_element_type=F32)
            state = state_ref[h]
            o = o + jnp.dot((q * xi_ref[h]).astype(BF16), state.astype(BF16), preferred_element_type=F32)
            kv = jnp.dot((k * zeta_ref[h]).T.astype(BF16), vb, preferred_element_type=F32)
            state_ref[h] = math.exp(c * log_g) * state + kv
            mu = jnp.mean(o, axis=-1, keepdims=True)
            oc = o - mu
            var = jnp.mean(oc * oc, axis=-1, keepdims=True)
            y = oc * lax.rsqrt(var + LN_EPS) * gng_ref[:, sl] + gnb_ref[:, sl]
            gate = gate_ref[0, rows, sl]
            o_ref[0, rows, sl] = (gate * _sigmoid(gate) * y).astype(BF16)


def _rotary_tables(t):
    d = RET_HEAD_DIM
    ang = np.arange(t)[:, None] * ROPE_BASE ** (-np.arange(0, d, 2) / d)[None, :]
    cos, sin = np.cos(ang), np.sin(ang)
    return (jnp.asarray(np.concatenate([cos, cos], axis=1), dtype=F32),
            jnp.asarray(np.concatenate([-sin, sin], axis=1), dtype=F32))


def _retention(ret, gn_g, gn_b):
    bsz, t, _ = ret.shape
    c = RET_STEP
    cos2, sin2 = _rotary_tables(t)
    part = lambda j: pl.BlockSpec((1, c, RET_WIDTH), lambda b, n: (b, n, j))
    tab = pl.BlockSpec((c, RET_HEAD_DIM), lambda b, n: (n, 0))
    head_mat = pltpu.VMEM((RET_HEADS, RET_CHUNK, RET_HEAD_DIM), F32)
    return pl.pallas_call(
        _retention_kernel,
        out_shape=jax.ShapeDtypeStruct((bsz, t, RET_WIDTH), BF16),
        grid=(bsz, t // c),
        in_specs=[part(0), part(1), part(2), part(3), tab, tab,
                  _resident((1, RET_WIDTH)), _resident((1, RET_WIDTH))],
        out_specs=pl.BlockSpec((1, c, RET_WIDTH), lambda b, n: (b, n, 0)),
        scratch_shapes=[pltpu.VMEM((RET_HEADS, RET_HEAD_DIM, RET_HEAD_DIM), F32), head_mat, head_mat, head_mat],
        compiler_params=_params("parallel", "arbitrary"),
        name="retention",
    )(ret, ret, ret, ret, cos2, sin2, gn_g.reshape(1, -1), gn_b.reshape(1, -1))


def _tail_kernel(x_ref, on_ref, or_ref, p_ref, wn_ref, wr_ref, w13_ref, w2_ref, wp_ref, wg_ref,
                 g_ref, b_ref, o_ref):
    mix = (jnp.dot(on_ref[...], wn_ref[...], preferred_element_type=F32)
           + jnp.dot(or_ref[...], wr_ref[...], preferred_element_type=F32))
    x = _layer_norm(ALPHA * x_ref[...] + mix, g_ref[0:1, :], b_ref[0:1, :])
    x = _layer_norm(ALPHA * x + 0.5 * _swiglu(x.astype(BF16), w13_ref, w2_ref), g_ref[1:2, :], b_ref[1:2, :])
    e = (jnp.dot(p_ref[...].astype(BF16), wp_ref[...], preferred_element_type=F32)
         * _sigmoid(jnp.dot(x.astype(BF16), wg_ref[...], preferred_element_type=F32)))
    o_ref[...] = _layer_norm(ALPHA * x + e, g_ref[2:3, :], b_ref[2:3, :])


def _tail(x2d, o_nsa, o_ret, p2d, w_out, w13, w2, layer, w_ple, w_gate, g, b):
    n = x2d.shape[0]
    row = lambda w: pl.BlockSpec((ROW_TILE, w), lambda i: (i, 0))
    return pl.pallas_call(
        _tail_kernel,
        out_shape=jax.ShapeDtypeStruct((n, D_MODEL), F32),
        grid=(n // ROW_TILE,),
        in_specs=[row(D_MODEL), row(NSA_WIDTH), row(RET_WIDTH), row(PLE_DIM),
                  _resident((NSA_WIDTH, D_MODEL)), _resident((RET_WIDTH, D_MODEL)),
                  *_ffn_weight_specs(layer, 1),
                  _resident((PLE_DIM, D_MODEL)), _resident((D_MODEL, D_MODEL)),
                  _resident((3, D_MODEL)), _resident((3, D_MODEL))],
        out_specs=row(D_MODEL),
        compiler_params=_params("parallel"),
        name="tail",
    )(x2d, o_nsa, o_ret, p2d, w_out[:NSA_WIDTH].astype(BF16), w_out[NSA_WIDTH:].astype(BF16),
      w13, w2, w_ple.astype(BF16), w_gate.astype(BF16), g, b)


def _token_mixer(x, w_in, cmp_pos, cmp_w1, cmp_w2, gn_g, gn_b):
    t = x.shape[1]
    raw_k, raw_v, ret, k_aug, kw, qt, vs_t, vw_t, gt = _in_proj(x, w_in)
    kc, vct = _compress(raw_k, raw_v, cmp_pos, cmp_w1, cmp_w2)
    oct_, nsel = _nsa_select(qt, kc, vct)
    o_nsa = _nsa_main(qt, nsel, k_aug, kw, vs_t, vw_t, oct_, gt, min(512, t))
    o_ret = _retention(ret, gn_g, gn_b)
    return o_nsa, o_ret


def kernel(x, p, ffn_w13, ffn_w2, w_in, cmp_pos, cmp_w1, cmp_w2, ret_gn_g, ret_gn_b,
           w_out, w_ple, w_ple_gate, ln_g, ln_b):
    bsz, t, d = x.shape
    n = bsz * t
    w13, w2 = ffn_w13.astype(BF16), ffn_w2.astype(BF16)
    for i in range(ffn_w13.shape[0]):
        h = _ffn_ln(x.reshape(n, d), w13, w2, i, 0, ln_g[i, 0], ln_b[i, 0])
        o_nsa, o_ret = _token_mixer(h.reshape(bsz, t, d), w_in[i], cmp_pos[i], cmp_w1[i], cmp_w2[i],
                                    ret_gn_g[i], ret_gn_b[i])
        h = _tail(h, o_nsa.reshape(n, -1), o_ret.reshape(n, -1), p[i].reshape(n, -1), w_out[i],
                  w13, w2, i, w_ple[i], w_ple_gate[i], ln_g[i, 1:], ln_b[i, 1:])
        x = h.reshape(bsz, t, d)
    return x
```
